```python
import jax, jax.numpy as jnp
from jax import lax
import numpy as np

D_MODEL = 1024
BATCH = 8
SEQ = 2048
DEPTH = 2
DEC_BATCH = 128
DEC_SEQ = 4
PAST_LEN = 8192
PAGE_SIZE = 128

H_A = 8
NOPE = 64
ROPE_A = 32
V_A = 64
Q_LORA = 384
KV_LORA = 256
LAT = KV_LORA + ROPE_A
MLA_SCALE = (NOPE + ROPE_A) ** -0.5
H_B = 8
N_KV_B = 2
GRP = H_B // N_KV_B
D_B = 64
BLK_CMP = 32
BLK_SLC = 64
CMP_PER_SLC = BLK_SLC // BLK_CMP
N_SEL = 8
WINDOW = 512
FORCE_BONUS = 1e4
NSA_SCALE = D_B ** -0.5
ROPE_THETA = 10000.0
EPS = 1e-6
NEG = -1e30
Q_BLK = 128
WA = H_A * V_A
WB = H_B * D_B
IN_SPLITS = (Q_LORA, KV_LORA, ROPE_A, WA, WB, 6 * N_KV_B * D_B, 3 * H_B, WB, 2 * D_MODEL)
IN_COLS = Q_LORA + KV_LORA + ROPE_A + WA + WB + 6 * N_KV_B * D_B + 3 * H_B + WB + 2 * D_MODEL

kernel_name = "hybrid_mla_nsa_gated_decoder_step"


def rmsnorm(x, g):
    xf = x.astype(jnp.float32)
    y = xf * lax.rsqrt(jnp.mean(xf * xf, axis=-1, keepdims=True) + EPS)
    return (y * g.astype(jnp.float32)).astype(x.dtype)


def rope(x, pos):
    half = x.shape[-1] // 2
    inv = ROPE_THETA ** (-jnp.arange(half, dtype=jnp.float32) / half)
    ang = pos.astype(jnp.float32)[:, None] * inv[None, :]
    shape = (1, pos.shape[0]) + (1,) * (x.ndim - 3) + (half,)
    cos = jnp.cos(ang).reshape(shape).astype(x.dtype)
    sin = jnp.sin(ang).reshape(shape).astype(x.dtype)
    x1, x2 = x[..., :half], x[..., half:]
    return jnp.concatenate([x1 * cos - x2 * sin, x2 * cos + x1 * sin], axis=-1)


def masked_softmax(s, mask):
    s = jnp.where(mask, s.astype(jnp.float32), NEG)
    return jax.nn.softmax(s, axis=-1) * mask


def split_cols(z):
    out, start = [], 0
    for n in IN_SPLITS:
        out.append(z[..., start:start + n])
        start += n
    return out


def gather_pages(pool, l, page_table):
    g = pool[l, page_table]
    return g.reshape((g.shape[0], g.shape[1] * g.shape[2]) + g.shape[3:])


def over_query_blocks(fn, qs, pos):
    T = pos.shape[0]
    nb = T // Q_BLK
    def to_blocks(a):
        return jnp.moveaxis(a.reshape((a.shape[0], nb, Q_BLK) + a.shape[2:]), 1, 0)
    out = lax.map(lambda args: fn(*args), tuple(to_blocks(a) for a in qs) + (pos.reshape(nb, Q_BLK),))
    out = jnp.moveaxis(out, 0, 1)
    return out.reshape((out.shape[0], T) + out.shape[3:])


def mla_attend(q_nope, q_rope, q_pos, lat_all, w_uk, w_uv):
    ckv, kr = lat_all[..., :KV_LORA], lat_all[..., KV_LORA:]
    q_lat = jnp.einsum('bqhn,chn->bqhc', q_nope, w_uk)
    s = (jnp.einsum('bqhc,bkc->bhqk', q_lat, ckv) + jnp.einsum('bqhr,bkr->bhqk', q_rope, kr)) * MLA_SCALE
    ok = jnp.arange(lat_all.shape[1])[None, :] <= q_pos[:, None]
    p = masked_softmax(s, ok[None, None]).astype(ckv.dtype)
    o_lat = jnp.einsum('bhqk,bkc->bqhc', p, ckv)
    return jnp.einsum('bqhc,chv->bqhv', o_lat, w_uv)


def compress_blocks(kv, pe):
    B, Tk = kv.shape[:2]
    nc = Tk // BLK_CMP
    blk = kv[:, :nc * BLK_CMP].reshape(B, nc, BLK_CMP, 2, N_KV_B, D_B).astype(jnp.float32)
    w = jnp.transpose(1.0 + pe.astype(jnp.float32), (1, 0, 2))[:, :, None, :]
    kvc = jnp.mean(blk * w, axis=2).astype(kv.dtype)
    return kvc[:, :, 0], kvc[:, :, 1]


def selection_blocks(kv):
    B, Tk = kv.shape[:2]
    ns = -(-Tk // BLK_SLC)
    kv = jnp.pad(kv, ((0, 0), (0, ns * BLK_SLC - Tk), (0, 0), (0, 0), (0, 0)))
    kv = jnp.transpose(kv.reshape(B, ns, BLK_SLC, 2, N_KV_B, D_B), (3, 0, 4, 1, 2, 5))
    return kv[0], kv[1]


def nsa_attend(q, g, q_pos, kc, vc, ks, vs):
    B, Tq = q.shape[:2]
    nc, ns = kc.shape[1], ks.shape[2]
    qg = q.reshape(B, Tq, N_KV_B, GRP, D_B)
    s_c = jnp.einsum('bqgrd,bcgd->bqgrc', qg, kc) * NSA_SCALE
    c_ok = (jnp.arange(1, nc + 1) * BLK_CMP - 1)[None, :] <= q_pos[:, None]
    p_c = masked_softmax(s_c, c_ok[None, :, None, None, :])
    o_c = jnp.einsum('bqgrc,bcgd->bqgrd', p_c.astype(vc.dtype), vc)
    imp = jnp.pad(p_c.sum(3), ((0, 0), (0, 0), (0, 0), (0, ns * CMP_PER_SLC - nc)))
    imp = imp.reshape(B, Tq, N_KV_B, ns, CMP_PER_SLC).sum(-1)
    blk = jnp.arange(ns)[None, :]
    cur = (q_pos // BLK_SLC)[:, None]
    forced = (blk == 0) | (blk == cur) | (blk == cur - 1)
    score = jnp.where((blk <= cur)[None, :, None, :],
                      imp + jnp.where(forced, FORCE_BONUS, 0.0)[None, :, None, :], NEG)
    n_sel = min(N_SEL, ns)
    _, idx = lax.top_k(score, n_sel)
    bi = jnp.arange(B)[:, None, None, None]
    gi = jnp.arange(N_KV_B)[None, None, :, None]
    kb = ks[bi, gi, idx].reshape(B, Tq, N_KV_B, n_sel * BLK_SLC, D_B)
    vb = vs[bi, gi, idx].reshape(B, Tq, N_KV_B, n_sel * BLK_SLC, D_B)
    kpos = (idx[..., None] * BLK_SLC + jnp.arange(BLK_SLC)).reshape(B, Tq, N_KV_B, n_sel * BLK_SLC)
    s_ok = kpos <= q_pos[None, :, None, None]
    s_s = jnp.einsum('bqgrd,bqgkd->bqgrk', qg, kb) * NSA_SCALE
    p_s = masked_softmax(s_s, s_ok[:, :, :, None, :])
    o_s = jnp.einsum('bqgrk,bqgkd->bqgrd', p_s.astype(vb.dtype), vb)
    return (o_c.reshape(B, Tq, H_B, D_B) * g[..., 0:1] + o_s.reshape(B, Tq, H_B, D_B) * g[..., 1:2])


def window_attend(qg, q_pos, k, v, k_pos):
    ok = ((k_pos[None, :] <= q_pos[:, None]) & (k_pos[None, :] > q_pos[:, None] - WINDOW)
          & (k_pos[None, :] >= 0))
    s = jnp.einsum('bqgrd,bkgd->bgrqk', qg, k) * NSA_SCALE
    p = masked_softmax(s, ok[None, None, None])
    return jnp.einsum('bgrqk,bkgd->bqgrd', p.astype(v.dtype), v)


def window_prompt(q, pos, kv):
    B, S = q.shape[:2]
    nb = S // Q_BLK
    kvp = jnp.pad(kv, ((0, 0), (WINDOW, 0), (0, 0), (0, 0), (0, 0)))
    idx = jnp.arange(nb)[:, None] * Q_BLK + jnp.arange(Q_BLK + WINDOW)[None, :]
    kvw = kvp[:, idx]
    k_pos = idx - WINDOW
    q_blk = q.reshape(B, nb, Q_BLK, N_KV_B, GRP, D_B)
    o = jax.vmap(window_attend, in_axes=(1, 0, 1, 1, 0), out_axes=1)(
        q_blk, pos.reshape(nb, Q_BLK), kvw[:, :, :, 0], kvw[:, :, :, 1], k_pos)
    return o.reshape(B, S, H_B, D_B)


def layer(x, c, pos, w, past=None):
    (norm_g, w_ada, b_ada, w_in, q_norm_g, kv_norm_g, w_uq, w_uk, w_uv, pe, w_pa, w_pb, w_out) = w
    B, T, _ = x.shape
    mod = jax.nn.silu(c) @ w_ada + b_ada
    shift, scale, gate = jnp.split(mod[:, None, :], 3, axis=-1)
    u = rmsnorm(x, norm_g) * (1 + scale) + shift
    cq, ckv, kr, za, qb, kvb, gbr, zb, gm = split_cols(u @ w_in)

    q = (rmsnorm(cq, q_norm_g) @ w_uq).reshape(B, T, H_A, NOPE + ROPE_A)
    q_nope, q_rope = q[..., :NOPE], rope(q[..., NOPE:], pos)
    mla_rows = jnp.concatenate([rmsnorm(ckv, kv_norm_g), rope(kr, pos)], axis=-1)

    qb = rope(qb.reshape(B, T, H_B, D_B), pos)
    kvb = kvb.reshape(B, T, 3, 2, N_KV_B, D_B)
    k_rot = rope(kvb[:, :, :, 0].reshape(B, T, 3 * N_KV_B, D_B), pos).reshape(B, T, 3, N_KV_B, D_B)
    kvb = jnp.stack([k_rot, kvb[:, :, :, 1]], axis=3)
    cmp_rows, slc_rows, win_rows = kvb[:, :, 0], kvb[:, :, 1], kvb[:, :, 2]
    g_br = jax.nn.sigmoid(gbr.reshape(B, T, H_B, 3))

    if past is None:
        lat_all, cmp_all, slc_all = mla_rows, cmp_rows, slc_rows
    else:
        lat_all = jnp.concatenate([past[0], mla_rows], axis=1)
        cmp_all = jnp.concatenate([past[1], cmp_rows], axis=1)
        slc_all = jnp.concatenate([past[2], slc_rows], axis=1)
    kc, vc = compress_blocks(cmp_all, pe)
    ks, vs = selection_blocks(slc_all)
    mla_fn = lambda qn, qr, p: mla_attend(qn, qr, p, lat_all, w_uk, w_uv)
    nsa_fn = lambda qq, gg, p: nsa_attend(qq, gg, p, kc, vc, ks, vs)

    if past is None:
        o_a = over_query_blocks(mla_fn, (q_nope, q_rope), pos)
        o_b = over_query_blocks(nsa_fn, (qb, g_br), pos)
        o_w = window_prompt(qb, pos, win_rows)
        win_state = win_rows[:, -min(WINDOW, T):]
    else:
        o_a = mla_fn(q_nope, q_rope, pos)
        o_b = nsa_fn(qb, g_br, pos)
        wbuf = past[3].shape[1]
        win_all = jnp.concatenate([past[3], win_rows], axis=1)
        k_pos = pos[0] + jnp.arange(-wbuf, T)
        o_w = window_attend(qb.reshape(B, T, N_KV_B, GRP, D_B), pos,
                            win_all[:, :, 0], win_all[:, :, 1], k_pos).reshape(B, T, H_B, D_B)
        win_state = win_all[:, -wbuf:]
    o_nsa = o_b + g_br[..., 2:3] * o_w

    y_a = (o_a.reshape(B, T, WA) * jax.nn.silu(za)) @ w_pa
    y_b = (o_nsa.reshape(B, T, WB) * jax.nn.silu(zb)) @ w_pb
    g_a, g_b = jnp.split(jax.nn.sigmoid(gm), 2, axis=-1)
    out = (g_a * y_a + g_b * y_b) @ w_out
    return x + gate * out, mla_rows, cmp_rows, slc_rows, win_state


def setup_inputs(seed: int = 0) -> dict:
    key = jax.random.key(seed)
    ks = jax.random.split(key, 24)
    n_pages = PAST_LEN // PAGE_SIZE
    n_used = DEC_BATCH * n_pages
    n_pool = (5 * n_used + 3) // 4
    win_buf = min(WINDOW, PAST_LEN)
    def nrm(k, shape, s):
        return jax.random.normal(k, shape, jnp.float32) * s
    return {
        "x_prompt": nrm(ks[0], (BATCH, SEQ, D_MODEL), 1.0),
        "x_sample": nrm(ks[1], (DEC_BATCH, DEC_SEQ, D_MODEL), 1.0),
        "c_prompt": nrm(ks[2], (BATCH, D_MODEL), 1.0),
        "c_sample": nrm(ks[3], (DEC_BATCH, D_MODEL), 1.0),
        "cache_mla": nrm(ks[4], (DEPTH, n_pool, PAGE_SIZE, LAT), 1.0),
        "cache_nsa_cmp": nrm(ks[5], (DEPTH, n_pool, PAGE_SIZE, 2, N_KV_B, D_B), 1.0),
        "cache_nsa_slc": nrm(ks[6], (DEPTH, n_pool, PAGE_SIZE, 2, N_KV_B, D_B), 1.0),
        "state_nsa_win": nrm(ks[7], (DEPTH, DEC_BATCH, win_buf, 2, N_KV_B, D_B), 1.0),
        "page_table": jax.random.permutation(ks[8], n_pool)[:n_used].reshape(DEC_BATCH, n_pages).astype(jnp.int32),
        "norm_g": 1.0 + nrm(ks[9], (DEPTH, D_MODEL), 0.01),
        "w_ada": nrm(ks[10], (DEPTH, D_MODEL, 3 * D_MODEL), 0.5 * D_MODEL ** -0.5),
        "b_ada": nrm(ks[11], (DEPTH, 3 * D_MODEL), 0.02),
        "w_in": nrm(ks[12], (DEPTH, D_MODEL, IN_COLS), D_MODEL ** -0.5),
        "q_norm_g": 1.0 + nrm(ks[13], (DEPTH, Q_LORA), 0.01),
        "kv_norm_g": 1.0 + nrm(ks[14], (DEPTH, KV_LORA), 0.01),
        "w_uq": nrm(ks[15], (DEPTH, Q_LORA, H_A * (NOPE + ROPE_A)), Q_LORA ** -0.5),
        "w_uk": nrm(ks[16], (DEPTH, KV_LORA, H_A, NOPE), KV_LORA ** -0.5),
        "w_uv": nrm(ks[17], (DEPTH, KV_LORA, H_A, V_A), KV_LORA ** -0.5),
        "nsa_pe": nrm(ks[18], (DEPTH, 2, BLK_CMP, D_B), 0.1),
        "w_pa": nrm(ks[19], (DEPTH, WA, D_MODEL), WA ** -0.5),
        "w_pb": nrm(ks[20], (DEPTH, WB, D_MODEL), WB ** -0.5),
        "w_out": nrm(ks[21], (DEPTH, D_MODEL, D_MODEL), D_MODEL ** -0.5),
        "final_g": 1.0 + nrm(ks[22], (D_MODEL,), 0.01),
    }


def reference(x_prompt, x_sample, c_prompt, c_sample, cache_mla, cache_nsa_cmp, cache_nsa_slc,
              state_nsa_win, page_table, norm_g, w_ada, b_ada, w_in, q_norm_g, kv_norm_g,
              w_uq, w_uk, w_uv, nsa_pe, w_pa, w_pb, w_out, final_g):
    past_len = page_table.shape[1] * cache_mla.shape[2]
    pos_p = jnp.arange(x_prompt.shape[1], dtype=jnp.int32)
    pos_s = past_len + jnp.arange(x_sample.shape[1], dtype=jnp.int32)
    xp, xs = x_prompt, x_sample
    mla_p, cmp_p, slc_p, win_p = [], [], [], []
    mla_s, cmp_s, slc_s, win_s = [], [], [], []
    for l in range(DEPTH):
        w = (norm_g[l], w_ada[l], b_ada[l], w_in[l], q_norm_g[l], kv_norm_g[l], w_uq[l], w_uk[l],
             w_uv[l], nsa_pe[l], w_pa[l], w_pb[l], w_out[l])
        xp, r_mla, r_cmp, r_slc, r_win = layer(xp, c_prompt, pos_p, w)
        mla_p.append(r_mla); cmp_p.append(r_cmp); slc_p.append(r_slc); win_p.append(r_win)
        past = (gather_pages(cache_mla, l, page_table), gather_pages(cache_nsa_cmp, l, page_table),
                gather_pages(cache_nsa_slc, l, page_table), state_nsa_win[l])
        xs, r_mla, r_cmp, r_slc, r_win = layer(xs, c_sample, pos_s, w, past)
        mla_s.append(r_mla); cmp_s.append(r_cmp); slc_s.append(r_slc); win_s.append(r_win)
    y_prompt = rmsnorm(xp, final_g)
    y_sample = rmsnorm(xs, final_g)
    return (y_prompt, y_sample,
            jnp.stack(mla_p), jnp.stack(cmp_p), jnp.stack(slc_p), jnp.stack(win_p),
            jnp.stack(mla_s), jnp.stack(cmp_s), jnp.stack(slc_s), jnp.stack(win_s))
```

```python
import functools

import jax
import jax.numpy as jnp
from jax import lax
from jax.experimental import pallas as pl
from jax.experimental.pallas import tpu as pltpu

F32 = jnp.float32
BF16 = jnp.bfloat16

D_MODEL = 1024
H_A = 8
NOPE = 64
ROPE_A = 32
V_A = 64
Q_LORA = 384
KV_LORA = 256
LAT = KV_LORA + ROPE_A
MLA_SCALE = (NOPE + ROPE_A) ** -0.5
H_B = 8
N_KV_B = 2
GRP = H_B // N_KV_B
D_B = 64
BLK_CMP = 32
BLK_SLC = 64
N_SEL = 8
WINDOW = 512
FORCE_BONUS = 1e4
NSA_SCALE = D_B ** -0.5
ROPE_THETA = 10000.0
EPS = 1e-6
NEG = -1e30
LOWEST = -3e38
WA = H_A * V_A
WB = H_B * D_B
KVW = 2 * N_KV_B * D_B

LANES = 128
SUBLANES = 8
VMEM_LIMIT = 56 * 1024 * 1024

TM = 256
TQ = 256
DEC_ROWS = 8
PAGES_PER_CHUNK = 16

C_CQ = 0
C_CKV = Q_LORA
C_KR = C_CKV + KV_LORA
C_QB = C_KR + LANES
C_KVB = C_QB + WB
C_GBR = C_KVB + 3 * KVW
C_END = C_GBR + LANES


def _dot(a, b):
    return jnp.dot(a, b, preferred_element_type=F32)


def _dot_nt(a, b):
    return lax.dot_general(a, b, (((1,), (1,)), ((), ())), preferred_element_type=F32)


def _cparams(sem):
    return pltpu.CompilerParams(dimension_semantics=sem, vmem_limit_bytes=VMEM_LIMIT)


def _rms(x, g):
    return x * lax.rsqrt(jnp.mean(x * x, axis=-1, keepdims=True) + EPS) * g


def _rope_lanes(x, cos, sin_signed, half):
    lane = lax.broadcasted_iota(jnp.int32, x.shape, 1)
    first = (lane % (2 * half)) < half
    swapped = jnp.where(first, pltpu.roll(x, LANES - half, 1), pltpu.roll(x, half, 1))
    return x * cos + swapped * sin_signed


def _modulated(x, mod, ng):
    shift = mod[:, 0:D_MODEL]
    scale = mod[:, D_MODEL:2 * D_MODEL]
    return _rms(x, ng) * (1.0 + scale) + shift


def _mod_kernel(c_ref, w_ref, b_ref, o_ref):
    c = c_ref[...]
    o_ref[...] = _dot(jax.nn.silu(c).astype(BF16), w_ref[...]) + b_ref[...]


def _mod_call(c_all, w_ada, b_ada):
    depth = w_ada.shape[0]
    rows = c_all.shape[0]
    nb = 3 * D_MODEL // D_MODEL
    return pl.pallas_call(
        _mod_kernel,
        grid=(depth, nb),
        in_specs=[
            pl.BlockSpec((rows, D_MODEL), lambda l, j: (0, 0)),
            pl.BlockSpec((None, D_MODEL, D_MODEL), lambda l, j: (l, 0, j)),
            pl.BlockSpec((None, 1, D_MODEL), lambda l, j: (l, 0, j)),
        ],
        out_specs=pl.BlockSpec((None, rows, D_MODEL), lambda l, j: (l, 0, j)),
        out_shape=jax.ShapeDtypeStruct((depth, rows, 3 * D_MODEL), F32),
        compiler_params=_cparams(("arbitrary", "arbitrary")),
        name="mod",
    )(c_all, w_ada, b_ada)


def _proj_kernel(x_ref, mod_ref, rope_ref, ng_ref, qng_ref, kvng_ref, win_ref, wuq_ref, wk1_ref,
                 wk2_ref, wv_ref, pew_ref, q_out, mla_out, kmla_out, vmla_out, qb_out, cmp_out,
                 slc_out, winr_out, gbr_out, *maybe_kvc_out):
    x = x_ref[...]
    ub = _modulated(x, mod_ref[...], ng_ref[...]).astype(BF16)

    cq = _dot(ub, win_ref[:, C_CQ:C_CKV])
    q = _dot(_rms(cq, qng_ref[...]).astype(BF16), wuq_ref[...])
    cos_q, sin_q = rope_ref[0], rope_ref[1]
    for h in range(H_A):
        blk = q[:, h * LANES:(h + 1) * LANES]
        q_out[:, h * LANES:(h + 1) * LANES] = _rope_lanes(blk, cos_q, sin_q, ROPE_A // 2).astype(BF16)

    zl = _dot(ub, win_ref[:, C_CKV:C_QB])
    ckvn = _rms(zl[:, 0:KV_LORA], kvng_ref[...])
    krr = _rope_lanes(zl[:, KV_LORA:KV_LORA + LANES], rope_ref[2], rope_ref[3], ROPE_A // 2)
    mla_out[:, 0:KV_LORA] = ckvn
    mla_out[:, KV_LORA:LAT] = krr[:, 0:ROPE_A]
    ckvb = ckvn.astype(BF16)
    kmla_out[...] = (_dot(ckvb, wk1_ref[...]) + _dot(krr.astype(BF16), wk2_ref[...])).astype(BF16)
    vmla_out[...] = _dot(ckvb, wv_ref[...]).astype(BF16)

    cos_b, sin_b = rope_ref[4], rope_ref[5]
    zq = _dot(ub, win_ref[:, C_QB:C_KVB])
    for j in range(WB // LANES):
        blk = zq[:, j * LANES:(j + 1) * LANES]
        qb_out[:, j * LANES:(j + 1) * LANES] = _rope_lanes(blk, cos_b, sin_b, D_B // 2) * NSA_SCALE
    zkv = _dot(ub, win_ref[:, C_KVB:C_GBR])
    for j, ref in enumerate((cmp_out, slc_out, winr_out)):
        k = zkv[:, j * KVW:j * KVW + LANES]
        ref[:, 0:LANES] = _rope_lanes(k, cos_b, sin_b, D_B // 2)
        ref[:, LANES:KVW] = zkv[:, j * KVW + LANES:(j + 1) * KVW]
    gbr_out[...] = jax.nn.sigmoid(_dot(ub, win_ref[:, C_GBR:C_END]))

    if maybe_kvc_out:
        (kvc_out,) = maybe_kvc_out
        tm = x.shape[0]
        rows = cmp_out[...].reshape(tm // BLK_CMP, BLK_CMP, KVW) * pew_ref[...][None]
        kvc_out[...] = jnp.sum(rows, axis=1) * (1.0 / BLK_CMP)


def _proj_call(x, mod, rope_tab, w, *, tokens_per_mod_row, rope_period, with_kvc):
    n = x.shape[0]
    tm = min(TM, n)
    assert n % tm == 0 and rope_period % tm == 0
    nrope = rope_period // tm
    if tokens_per_mod_row is None:
        mod_spec = pl.BlockSpec((tm, 3 * D_MODEL), lambda i: (i, 0))
    else:
        assert tokens_per_mod_row % tm == 0
        per = tokens_per_mod_row // tm
        mod_spec = pl.BlockSpec((None, 1, 3 * D_MODEL), lambda i: (i // per, 0, 0))
    full = lambda a: pl.BlockSpec(a.shape, lambda i: (0,) * a.ndim)
    row = lambda width: pl.BlockSpec((tm, width), lambda i: (i, 0))
    out_specs = [row(H_A * LANES), row(LAT), row(H_A * LANES), row(WA), row(WB), row(KVW), row(KVW),
                 row(KVW), row(LANES)]
    out_shape = [jax.ShapeDtypeStruct((n, H_A * LANES), BF16), jax.ShapeDtypeStruct((n, LAT), F32),
                 jax.ShapeDtypeStruct((n, H_A * LANES), BF16), jax.ShapeDtypeStruct((n, WA), BF16),
                 jax.ShapeDtypeStruct((n, WB), F32), jax.ShapeDtypeStruct((n, KVW), F32),
                 jax.ShapeDtypeStruct((n, KVW), F32), jax.ShapeDtypeStruct((n, KVW), F32),
                 jax.ShapeDtypeStruct((n, LANES), F32)]
    if with_kvc:
        out_specs.append(pl.BlockSpec((tm // BLK_CMP, KVW), lambda i: (i, 0)))
        out_shape.append(jax.ShapeDtypeStruct((n // BLK_CMP, KVW), F32))
    weights = (w["ng"], w["qng"], w["kvng"], w["win"], w["wuq"], w["wk1"], w["wk2"], w["wv"], w["pew"])
    return pl.pallas_call(
        _proj_kernel,
        grid=(n // tm,),
        in_specs=[row(D_MODEL), mod_spec,
                  pl.BlockSpec((6, tm, LANES), lambda i: (0, i % nrope, 0))] + [full(a) for a in weights],
        out_specs=out_specs,
        out_shape=out_shape,
        compiler_params=_cparams(("arbitrary",)),
        name="proj",
    )(x, mod, rope_tab, *weights)


def _flash_init(m_ref, l_ref, acc_ref):
    m_ref[...] = jnp.full(m_ref.shape, NEG, F32)
    l_ref[...] = jnp.zeros(l_ref.shape, F32)
    acc_ref[...] = jnp.zeros(acc_ref.shape, F32)


def _lane_tile(a, width):
    reps = width // a.shape[-1]
    return a if reps == 1 else jnp.concatenate([a] * reps, axis=-1)


def _flash_update(s, mask, v, m_ref, l_ref, acc_ref, idx):
    if mask is not None:
        s = jnp.where(mask, s, NEG)
    m_prev = m_ref[idx]
    l_prev = l_ref[idx]
    m_next = jnp.maximum(m_prev, jnp.max(s, axis=-1, keepdims=True))
    p = jnp.exp(s - _lane_tile(m_next, s.shape[-1]))
    if mask is not None:
        p = jnp.where(mask, p, 0.0)
    alpha = jnp.exp(m_prev - m_next)
    l_ref[idx] = alpha * l_prev + jnp.sum(p, axis=-1, keepdims=True)
    acc_ref[idx] = acc_ref[idx] * _lane_tile(alpha, v.shape[-1]) + _dot(p.astype(BF16), v)
    m_ref[idx] = m_next


def _flash_result(l_ref, acc_ref, idx):
    l = l_ref[idx]
    l = jnp.where(l == 0.0, 1.0, l)
    return acc_ref[idx] / _lane_tile(l, acc_ref.shape[-1])


def _group_queries(qb, g, rows):
    lane = lax.broadcasted_iota(jnp.int32, (rows, LANES), 1)
    keep = (lane // D_B) == g
    parts = []
    for r in range(GRP):
        h = g * GRP + r
        blk = qb[:, (h // 2) * LANES:(h // 2 + 1) * LANES]
        if h % 2 != g:
            blk = pltpu.roll(blk, D_B, 1)
        parts.append(jnp.where(keep, blk, 0.0))
    return jnp.concatenate(parts, axis=0).astype(BF16)


def _group_gates(gbr, g, j, rows):
    parts = []
    for r in range(GRP):
        c = 3 * (g * GRP + r) + j
        parts.append(jnp.broadcast_to(gbr[:, c:c + 1], (rows, LANES)))
    return jnp.concatenate(parts, axis=0)


def _ungroup(o_groups, rows):
    lane = lax.broadcasted_iota(jnp.int32, (rows, LANES), 1)
    blocks = []
    for j in range(WB // LANES):
        g = (2 * j) // GRP
        r0 = (2 * j) % GRP
        p0 = o_groups[g][r0 * rows:(r0 + 1) * rows]
        p1 = o_groups[g][(r0 + 1) * rows:(r0 + 2) * rows]
        if g == 1:
            p0 = pltpu.roll(p0, D_B, 1)
        else:
            p1 = pltpu.roll(p1, D_B, 1)
        blocks.append(jnp.where(lane < D_B, p0, p1))
    return blocks


def _compressed_attention(qg, kvc, qpos, n_cmp):
    nc = kvc.shape[0]
    s = _dot_nt(qg, kvc[:, 0:LANES].astype(BF16))
    c = lax.broadcasted_iota(jnp.int32, s.shape, 1)
    ok = ((c + 1) * BLK_CMP - 1 <= qpos) & (c < n_cmp)
    s = jnp.where(ok, s, NEG)
    p = jnp.where(ok, jnp.exp(s - jnp.max(s, axis=-1, keepdims=True)), 0.0)
    den = jnp.sum(p, axis=-1, keepdims=True)
    p = p / jnp.where(den == 0.0, 1.0, den)
    return p, _dot(p.astype(BF16), kvc[:, LANES:KVW].astype(BF16))


def _select_blocks(imp_cmp, cur, n_blocks):
    nc = imp_cmp.shape[-1]
    lane = lax.broadcasted_iota(jnp.int32, imp_cmp.shape, 1)
    pair = imp_cmp + pltpu.roll(imp_cmp, nc - 1, 1)
    blk = lane // 2
    forced = (blk == 0) | (blk == cur - 1)
    score = jnp.where(blk <= cur, pair + jnp.where(forced, FORCE_BONUS, 0.0), NEG)
    cand = ((lane % 2) == 0) & (blk < n_blocks) & (blk != cur)
    score = jnp.where(cand, score, LOWEST)
    lanef = lane.astype(F32)
    sel = jnp.zeros(imp_cmp.shape, F32)
    for _ in range(N_SEL - 1):
        top = jnp.max(score, axis=-1, keepdims=True)
        first = jnp.min(jnp.where(score == top, lanef, float(nc)), axis=-1, keepdims=True)
        hit = lanef == first
        sel = jnp.where(hit, 1.0, sel)
        score = jnp.where(hit, LOWEST, score)
    return sel


def _selection_matrix(nc, k0, nkeys):
    i = lax.broadcasted_iota(jnp.int32, (nc, nkeys), 0)
    k = lax.broadcasted_iota(jnp.int32, (nc, nkeys), 1) + k0
    return jnp.where(((i % 2) == 0) & ((k // BLK_SLC) == (i // 2)), 1.0, 0.0).astype(BF16)


def _expand_selection(sel, sel_matrix):
    return _dot(sel.astype(BF16), sel_matrix)


def _attn_kernel(q_ref, kmla_ref, vmla_ref, qb_ref, gbr_ref, kvc_ref, slc_ref, win_ref,
                 oa_ref, onsa_ref, m_a, l_a, acc_a, qg_s, m_s, l_s, acc_s, m_w, l_w, acc_w,
                 *, seq_len):
    tq = q_ref.shape[0]
    bk = tq
    qi = pl.program_id(1)
    q0 = qi * tq
    qpos = q0 + lax.broadcasted_iota(jnp.int32, (tq, 1), 0)
    qpos4 = jnp.concatenate([qpos] * GRP, axis=0)
    kiota = lax.broadcasted_iota(jnp.int32, (1, bk), 1)

    _flash_init(m_a, l_a, acc_a)

    def mla_body(kb, carry):
        k0 = pl.multiple_of(kb * bk, bk)
        mask = (k0 + kiota) <= qpos
        for h in range(H_A):
            s = _dot_nt(q_ref[:, h * LANES:(h + 1) * LANES],
                        kmla_ref[pl.ds(k0, bk), h * LANES:(h + 1) * LANES]) * MLA_SCALE
            v = vmla_ref[pl.ds(k0, bk), (h // 2) * LANES:(h // 2 + 1) * LANES]
            _flash_update(s, mask, v, m_a, l_a, acc_a, h)
        return carry

    lax.fori_loop(0, qi + 1, mla_body, 0)
    lane = lax.broadcasted_iota(jnp.int32, (tq, LANES), 1)
    for j in range(WA // LANES):
        oa_ref[:, j * LANES:(j + 1) * LANES] = jnp.where(
            lane < V_A, _flash_result(l_a, acc_a, 2 * j), _flash_result(l_a, acc_a, 2 * j + 1))

    qb = qb_ref[...]
    gbr = gbr_ref[...]
    kvc = kvc_ref[...]
    n_cmp = seq_len // BLK_CMP
    n_slc = -(-seq_len // BLK_SLC)
    cur = qpos // BLK_SLC
    o_cmp = []
    sels = []
    for g in range(N_KV_B):
        qg = _group_queries(qb, g, tq)
        qg_s[g] = qg
        p, o = _compressed_attention(qg, kvc, qpos4, n_cmp)
        o_cmp.append(o)
        imp = p[0:tq]
        for r in range(1, GRP):
            imp = imp + p[r * tq:(r + 1) * tq]
        sel = _select_blocks(imp, cur, n_slc)
        clane = lax.broadcasted_iota(jnp.int32, sel.shape, 1)
        sels.append(jnp.where(clane == 2 * cur, 1.0, sel))

    _flash_init(m_s, l_s, acc_s)

    def slc_body(kb, carry):
        k0 = pl.multiple_of(kb * bk, bk)
        causal = (k0 + kiota) <= qpos4
        kblk = slc_ref[pl.ds(k0, bk), 0:LANES].astype(BF16)
        vblk = slc_ref[pl.ds(k0, bk), LANES:KVW].astype(BF16)
        sel_matrix = _selection_matrix(kvc.shape[0], k0, bk)
        for g in range(N_KV_B):
            chosen = _expand_selection(sels[g], sel_matrix)
            chosen4 = jnp.concatenate([chosen] * GRP, axis=0)
            s = _dot_nt(qg_s[g], kblk)
            _flash_update(s, (chosen4 > 0.5) & causal, vblk, m_s, l_s, acc_s, g)
        return carry

    lax.fori_loop(0, qi + 1, slc_body, 0)

    _flash_init(m_w, l_w, acc_w)

    def win_body(kb, carry):
        k0 = pl.multiple_of(kb * bk, bk)
        kpos = k0 + kiota
        mask = (kpos <= qpos4) & (kpos > qpos4 - WINDOW)
        kblk = win_ref[pl.ds(k0, bk), 0:LANES].astype(BF16)
        vblk = win_ref[pl.ds(k0, bk), LANES:KVW].astype(BF16)
        for g in range(N_KV_B):
            s = _dot_nt(qg_s[g], kblk)
            _flash_update(s, mask, vblk, m_w, l_w, acc_w, g)
        return carry

    lo = jnp.maximum(q0 - (WINDOW - 1), 0) // bk
    lax.fori_loop(lo, qi + 1, win_body, 0)

    o_groups = []
    for g in range(N_KV_B):
        o_groups.append(o_cmp[g] * _group_gates(gbr, g, 0, tq)
                        + _flash_result(l_s, acc_s, g) * _group_gates(gbr, g, 1, tq)
                        + _flash_result(l_w, acc_w, g) * _group_gates(gbr, g, 2, tq))
    for j, blk in enumerate(_ungroup(o_groups, tq)):
        onsa_ref[:, j * LANES:(j + 1) * LANES] = blk


def _attn_call(q, kmla, vmla, qb, gbr, kvc, slc, win, batch, seq_len):
    tq = min(TQ, seq_len)
    nq = seq_len // tq
    n = batch * seq_len
    nc_pad = kvc.shape[1]
    tile = lambda width: pl.BlockSpec((tq, width), lambda b, i: (b * nq + i, 0))
    whole = lambda rows, width: pl.BlockSpec((None, rows, width), lambda b, i: (b, 0, 0))
    stat = lambda heads, rows: pltpu.VMEM((heads, rows, LANES), F32)
    return pl.pallas_call(
        functools.partial(_attn_kernel, seq_len=seq_len),
        grid=(batch, nq),
        in_specs=[tile(H_A * LANES), whole(seq_len, H_A * LANES), whole(seq_len, WA), tile(WB),
                  tile(LANES), whole(nc_pad, KVW), whole(seq_len, KVW), whole(seq_len, KVW)],
        out_specs=[tile(WA), tile(WB)],
        out_shape=[jax.ShapeDtypeStruct((n, WA), F32), jax.ShapeDtypeStruct((n, WB), F32)],
        scratch_shapes=[stat(H_A, tq), stat(H_A, tq), stat(H_A, tq),
                        pltpu.VMEM((N_KV_B, GRP * tq, LANES), BF16),
                        stat(N_KV_B, GRP * tq), stat(N_KV_B, GRP * tq), stat(N_KV_B, GRP * tq),
                        stat(N_KV_B, GRP * tq), stat(N_KV_B, GRP * tq), stat(N_KV_B, GRP * tq)],
        compiler_params=_cparams(("arbitrary", "arbitrary")),
        name="attn",
    )(q, kmla, vmla, qb, gbr, kvc, slc, win)


def _page_copy(pt_ref, cache_ref, buf, sem, layer, b, chunk, slot, i, pages_per_chunk, page_rows):
    page = pt_ref[b, chunk * pages_per_chunk + i]
    return pltpu.make_async_copy(cache_ref.at[layer, page],
                                 buf.at[slot, pl.ds(i * page_rows, page_rows)], sem.at[slot])


def _stream_chunk(pt_ref, cache_ref, buf, sem, layer, pages_per_chunk):
    b = pl.program_id(0)
    c = pl.program_id(1)
    nb = pl.num_programs(0)
    ncnk = pl.num_programs(1)
    page_rows = cache_ref.shape[2]
    step = b * ncnk + c
    slot = step % 2

    def start(bb, cc, sl):
        for i in range(pages_per_chunk):
            _page_copy(pt_ref, cache_ref, buf, sem, layer, bb, cc, sl, i, pages_per_chunk, page_rows).start()

    @pl.when(step == 0)
    def _():
        start(b, c, slot)

    @pl.when(step + 1 < nb * ncnk)
    def _():
        last = c == ncnk - 1
        start(jnp.where(last, b + 1, b), jnp.where(last, 0, c + 1), 1 - slot)

    for i in range(pages_per_chunk):
        _page_copy(pt_ref, cache_ref, buf, sem, layer, b, c, slot, i, pages_per_chunk, page_rows).wait()
    return slot


def _new_row_mask(rows, n_new):
    j = lax.broadcasted_iota(jnp.int32, (rows, LANES), 1)
    t = lax.broadcasted_iota(jnp.int32, (rows, LANES), 0) % DEC_ROWS
    return (j <= t) & (j < n_new)


def _pad_rows(a, rows):
    return jnp.concatenate([a, jnp.zeros((rows - a.shape[0], a.shape[1]), a.dtype)], axis=0)


def _dec_mla_kernel(pt_ref, q_ref, wuk_ref, cache_ref, new_ref, wuv_ref, o_ref,
                    buf, sem, qlat, qrope, m_r, l_r, acc_r, *, layer, pages_per_chunk, n_new):
    c = pl.program_id(1)
    slot = _stream_chunk(pt_ref, cache_ref, buf, sem, layer, pages_per_chunk)
    rows = H_A * DEC_ROWS

    @pl.when(c == 0)
    def _():
        for h in range(H_A):
            qh = q_ref[:, h * LANES:(h + 1) * LANES]
            qlat[h * DEC_ROWS:(h + 1) * DEC_ROWS] = _dot(qh, wuk_ref[h])
            qrope[h * DEC_ROWS:(h + 1) * DEC_ROWS] = pltpu.roll(qh.astype(F32), LANES - NOPE, 1)
        _flash_init(m_r, l_r, acc_r)

    def scores(lat_b):
        return (_dot_nt(qlat[...].astype(BF16), lat_b[:, 0:KV_LORA])
                + _dot_nt(qrope[:, 0:ROPE_A].astype(BF16), lat_b[:, KV_LORA:LAT])) * MLA_SCALE

    lat = buf[slot].astype(BF16)
    _flash_update(scores(lat), None, lat[:, 0:KV_LORA], m_r, l_r, acc_r, 0)

    @pl.when(c == pl.num_programs(1) - 1)
    def _():
        latn = _pad_rows(new_ref[...], LANES).astype(BF16)
        _flash_update(scores(latn), _new_row_mask(rows, n_new), latn[:, 0:KV_LORA], m_r, l_r, acc_r, 0)
        o_lat = _flash_result(l_r, acc_r, 0)
        x = _dot(o_lat.astype(BF16), wuv_ref[...]).reshape(H_A, DEC_ROWS, WA)
        head = lax.broadcasted_iota(jnp.int32, x.shape, 0)
        col = lax.broadcasted_iota(jnp.int32, x.shape, 2)
        o_ref[...] = jnp.sum(jnp.where(col // V_A == head, x, 0.0), axis=0)


def _dec_specs(pages_per_chunk, page_rows, width):
    return [pltpu.VMEM((2, pages_per_chunk * page_rows, width), F32), pltpu.SemaphoreType.DMA((2,))]


def _dec_mla_call(page_table, q, wuk, cache, new_rows, wuv, layer, n_new):
    batch, n_pages = page_table.shape
    page_rows = cache.shape[2]
    ppc = min(PAGES_PER_CHUNK, n_pages)
    assert n_pages % ppc == 0
    rows = H_A * DEC_ROWS
    per_b = lambda r, w: pl.BlockSpec((None, r, w), lambda b, c, pt: (b, 0, 0))
    full = lambda a: pl.BlockSpec(a.shape, lambda b, c, pt: (0,) * a.ndim)
    grid_spec = pltpu.PrefetchScalarGridSpec(
        num_scalar_prefetch=1,
        grid=(batch, n_pages // ppc),
        in_specs=[per_b(DEC_ROWS, H_A * LANES), full(wuk), pl.BlockSpec(memory_space=pl.ANY),
                  per_b(DEC_ROWS, LAT), full(wuv)],
        out_specs=per_b(DEC_ROWS, WA),
        scratch_shapes=_dec_specs(ppc, page_rows, LAT) + [
            pltpu.VMEM((rows, KV_LORA), F32), pltpu.VMEM((rows, LANES), F32),
            pltpu.VMEM((1, rows, LANES), F32), pltpu.VMEM((1, rows, LANES), F32),
            pltpu.VMEM((1, rows, KV_LORA), F32)],
    )
    return pl.pallas_call(
        functools.partial(_dec_mla_kernel, layer=layer, pages_per_chunk=ppc, n_new=n_new),
        grid_spec=grid_spec,
        out_shape=jax.ShapeDtypeStruct((batch, DEC_ROWS, WA), F32),
        compiler_params=_cparams(("arbitrary", "arbitrary")),
        name="dec_mla",
    )(page_table, q, wuk, cache, new_rows, wuv)


def _dec_cmp_kernel(pt_ref, qb_ref, cache_ref, pew_ref, oc_ref, sel_ref, buf, sem, kvc_s,
                    *, layer, pages_per_chunk, past_len):
    c = pl.program_id(1)
    slot = _stream_chunk(pt_ref, cache_ref, buf, sem, layer, pages_per_chunk)
    chunk_rows = buf.shape[1]
    blocks = chunk_rows // BLK_CMP
    n_cmp = past_len // BLK_CMP

    @pl.when(c == 0)
    def _():
        kvc_s[...] = jnp.zeros(kvc_s.shape, F32)

    rows = buf[slot].reshape(blocks, BLK_CMP, KVW) * pew_ref[...][None]
    kvc_s[pl.ds(pl.multiple_of(c * blocks, SUBLANES), blocks)] = jnp.sum(rows, axis=1) * (1.0 / BLK_CMP)

    @pl.when(c == pl.num_programs(1) - 1)
    def _():
        qb = qb_ref[...]
        kvc = kvc_s[...]
        t = lax.broadcasted_iota(jnp.int32, (DEC_ROWS, 1), 0)
        qpos = past_len + t
        qpos4 = jnp.concatenate([qpos] * GRP, axis=0)
        cur = qpos // BLK_SLC
        for g in range(N_KV_B):
            p, o = _compressed_attention(_group_queries(qb, g, DEC_ROWS), kvc, qpos4, n_cmp)
            oc_ref[g] = o
            imp = p[0:DEC_ROWS]
            for r in range(1, GRP):
                imp = imp + p[r * DEC_ROWS:(r + 1) * DEC_ROWS]
            sel_ref[g] = _select_blocks(imp, cur, n_cmp // (BLK_SLC // BLK_CMP))


def _dec_cmp_call(page_table, qb, cache, pew, layer, past_len):
    batch, n_pages = page_table.shape
    page_rows = cache.shape[2]
    ppc = min(PAGES_PER_CHUNK, n_pages)
    assert n_pages % ppc == 0 and (ppc * page_rows // BLK_CMP) % SUBLANES == 0
    nc_pad = -(-(past_len // BLK_CMP) // LANES) * LANES
    per_b = lambda r, w: pl.BlockSpec((None, r, w), lambda b, c, pt: (b, 0, 0))
    per_bg = lambda r, w: pl.BlockSpec((None, N_KV_B, r, w), lambda b, c, pt: (b, 0, 0, 0))
    grid_spec = pltpu.PrefetchScalarGridSpec(
        num_scalar_prefetch=1,
        grid=(batch, n_pages // ppc),
        in_specs=[per_b(DEC_ROWS, WB), pl.BlockSpec(memory_space=pl.ANY),
                  pl.BlockSpec(pew.shape, lambda b, c, pt: (0, 0))],
        out_specs=[per_bg(GRP * DEC_ROWS, LANES), per_bg(DEC_ROWS, nc_pad)],
        scratch_shapes=_dec_specs(ppc, page_rows, KVW) + [pltpu.VMEM((nc_pad, KVW), F32)],
    )
    return pl.pallas_call(
        functools.partial(_dec_cmp_kernel, layer=layer, pages_per_chunk=ppc, past_len=past_len),
        grid_spec=grid_spec,
        out_shape=[jax.ShapeDtypeStruct((batch, N_KV_B, GRP * DEC_ROWS, LANES), F32),
                   jax.ShapeDtypeStruct((batch, N_KV_B, DEC_ROWS, nc_pad), F32)],
        compiler_params=_cparams(("arbitrary", "arbitrary")),
        name="dec_cmp",
    )(page_table, qb, cache, pew)


def _dec_slc_kernel(pt_ref, qb_ref, gbr_ref, sel_ref, oc_ref, cache_ref, slcn_ref, wins_ref, winn_ref,
                    o_ref, buf, sem, qg_s, m_s, l_s, acc_s, *, layer, pages_per_chunk, past_len, n_new):
    c = pl.program_id(1)
    slot = _stream_chunk(pt_ref, cache_ref, buf, sem, layer, pages_per_chunk)
    chunk_rows = buf.shape[1]
    rows = GRP * DEC_ROWS

    @pl.when(c == 0)
    def _():
        qb = qb_ref[...]
        for g in range(N_KV_B):
            qg_s[g] = _group_queries(qb, g, DEC_ROWS)
        _flash_init(m_s, l_s, acc_s)

    kv = buf[slot].astype(BF16)
    sel_matrix = _selection_matrix(sel_ref.shape[-1], c * chunk_rows, chunk_rows)
    for g in range(N_KV_B):
        chosen = _expand_selection(sel_ref[g], sel_matrix)
        chosen4 = jnp.concatenate([chosen] * GRP, axis=0)
        s = _dot_nt(qg_s[g], kv[:, 0:LANES])
        _flash_update(s, chosen4 > 0.5, kv[:, LANES:KVW], m_s, l_s, acc_s, g)

    @pl.when(c == pl.num_programs(1) - 1)
    def _():
        gbr = gbr_ref[...]
        new_mask = _new_row_mask(rows, n_new)
        slcn = _pad_rows(slcn_ref[...], LANES).astype(BF16)
        for g in range(N_KV_B):
            _flash_update(_dot_nt(qg_s[g], slcn[:, 0:LANES]), new_mask, slcn[:, LANES:KVW],
                          m_s, l_s, acc_s, g)
        o_sel = [_flash_result(l_s, acc_s, g) for g in range(N_KV_B)]

        wbuf = wins_ref.shape[0]
        wins = wins_ref[...].astype(BF16)
        winn = _pad_rows(winn_ref[...], LANES).astype(BF16)
        j = lax.broadcasted_iota(jnp.int32, (rows, wbuf), 1)
        t = lax.broadcasted_iota(jnp.int32, (rows, wbuf), 0) % DEC_ROWS
        kpos = past_len - wbuf + j
        state_mask = (kpos > past_len + t - WINDOW) & (kpos >= 0)
        _flash_init(m_s, l_s, acc_s)
        o_groups = []
        for g in range(N_KV_B):
            _flash_update(_dot_nt(qg_s[g], wins[:, 0:LANES]), state_mask, wins[:, LANES:KVW],
                          m_s, l_s, acc_s, g)
            _flash_update(_dot_nt(qg_s[g], winn[:, 0:LANES]), new_mask, winn[:, LANES:KVW],
                          m_s, l_s, acc_s, g)
            o_groups.append(oc_ref[g] * _group_gates(gbr, g, 0, DEC_ROWS)
                            + o_sel[g] * _group_gates(gbr, g, 1, DEC_ROWS)
                            + _flash_result(l_s, acc_s, g) * _group_gates(gbr, g, 2, DEC_ROWS))
        for jb, blk in enumerate(_ungroup(o_groups, DEC_ROWS)):
            o_ref[:, jb * LANES:(jb + 1) * LANES] = blk


def _dec_slc_call(page_table, qb, gbr, sel, o_cmp, cache, slc_new, win_state, win_new, layer, past_len,
                  n_new):
    batch, n_pages = page_table.shape
    page_rows = cache.shape[2]
    ppc = min(PAGES_PER_CHUNK, n_pages)
    wbuf = win_state.shape[2]
    nc_pad = sel.shape[-1]
    rows = GRP * DEC_ROWS
    per_b = lambda r, w: pl.BlockSpec((None, r, w), lambda b, c, pt: (b, 0, 0))
    per_bg = lambda r, w: pl.BlockSpec((None, N_KV_B, r, w), lambda b, c, pt: (b, 0, 0, 0))
    stat = pltpu.VMEM((N_KV_B, rows, LANES), F32)
    grid_spec = pltpu.PrefetchScalarGridSpec(
        num_scalar_prefetch=1,
        grid=(batch, n_pages // ppc),
        in_specs=[per_b(DEC_ROWS, WB), per_b(DEC_ROWS, LANES), per_bg(DEC_ROWS, nc_pad),
                  per_bg(rows, LANES), pl.BlockSpec(memory_space=pl.ANY), per_b(DEC_ROWS, KVW),
                  pl.BlockSpec((None, None, wbuf, KVW), lambda b, c, pt: (layer, b, 0, 0)),
                  per_b(DEC_ROWS, KVW)],
        out_specs=per_b(DEC_ROWS, WB),
        scratch_shapes=_dec_specs(ppc, page_rows, KVW) + [
            pltpu.VMEM((N_KV_B, rows, LANES), BF16), stat, stat, stat],
    )
    return pl.pallas_call(
        functools.partial(_dec_slc_kernel, layer=layer, pages_per_chunk=ppc, past_len=past_len,
                          n_new=n_new),
        grid_spec=grid_spec,
        out_shape=jax.ShapeDtypeStruct((batch, DEC_ROWS, WB), F32),
        compiler_params=_cparams(("arbitrary", "arbitrary")),
        name="dec_slc",
    )(page_table, qb, gbr, sel, o_cmp, cache, slc_new, win_state, win_new)


def _out_kernel(x_ref, mod_ref, oa_ref, onsa_ref, ng_ref, wg_ref, wpa_ref, wpb_ref, wout_ref, fg_ref,
                y_ref, *, final):
    x = x_ref[...]
    mod = mod_ref[...]
    ub = _modulated(x, mod, ng_ref[...]).astype(BF16)
    za = jax.nn.silu(_dot(ub, wg_ref[:, 0:WA]))
    zb = jax.nn.silu(_dot(ub, wg_ref[:, WA:WA + WB]))
    gm = jax.nn.sigmoid(_dot(ub, wg_ref[:, WA + WB:WA + WB + 2 * D_MODEL]))
    ya = _dot((oa_ref[...] * za).astype(BF16), wpa_ref[...])
    yb = _dot((onsa_ref[...] * zb).astype(BF16), wpb_ref[...])
    merged = gm[:, 0:D_MODEL] * ya + gm[:, D_MODEL:2 * D_MODEL] * yb
    xn = x + mod[:, 2 * D_MODEL:3 * D_MODEL] * _dot(merged.astype(BF16), wout_ref[...])
    y_ref[...] = _rms(xn, fg_ref[...]) if final else xn


def _out_call(x, mod, o_a, o_nsa, w, final_g, *, tokens_per_mod_row, final):
    n = x.shape[0]
    tm = min(TM, n)
    assert n % tm == 0
    if tokens_per_mod_row is None:
        mod_spec = pl.BlockSpec((tm, 3 * D_MODEL), lambda i: (i, 0))
    else:
        per = tokens_per_mod_row // tm
        mod_spec = pl.BlockSpec((None, 1, 3 * D_MODEL), lambda i: (i // per, 0, 0))
    full = lambda a: pl.BlockSpec(a.shape, lambda i: (0,) * a.ndim)
    row = lambda width: pl.BlockSpec((tm, width), lambda i: (i, 0))
    weights = (w["ng"], w["wg"], w["wpa"], w["wpb"], w["wout"], final_g)
    return pl.pallas_call(
        functools.partial(_out_kernel, final=final),
        grid=(n // tm,),
        in_specs=[row(D_MODEL), mod_spec, row(WA), row(WB)] + [full(a) for a in weights],
        out_specs=row(D_MODEL),
        out_shape=jax.ShapeDtypeStruct((n, D_MODEL), F32),
        compiler_params=_cparams(("arbitrary",)),
        name="out",
    )(x, mod, o_a, o_nsa, *weights)


def _pad_cols(a, width):
    return jnp.pad(a, ((0, 0), (0, width - a.shape[1])))


def _layer_weights(l, norm_g, w_in, q_norm_g, kv_norm_g, w_uq, w_uk, w_uv, nsa_pe, w_pa, w_pb, w_out):
    wi = w_in[l]
    o = 0
    parts = {}
    for name, n in (("cq", Q_LORA), ("ckv", KV_LORA), ("kr", ROPE_A), ("za", WA), ("qb", WB),
                    ("kvb", 3 * KVW), ("gbr", 3 * H_B), ("zb", WB), ("gm", 2 * D_MODEL)):
        parts[name] = wi[:, o:o + n]
        o += n
    win = jnp.concatenate([parts["cq"], parts["ckv"], _pad_cols(parts["kr"], LANES), parts["qb"],
                           parts["kvb"], _pad_cols(parts["gbr"], LANES)], axis=1).astype(BF16)
    wg = jnp.concatenate([parts["za"], parts["zb"], parts["gm"]], axis=1).astype(BF16)
    head_pad = lambda a: jnp.pad(a, ((0, 0), (0, 0), (0, LANES - a.shape[2]))).reshape(a.shape[0], H_A * LANES)
    wuq = head_pad(w_uq[l].reshape(Q_LORA, H_A, NOPE + ROPE_A)).astype(BF16)
    wk1 = head_pad(w_uk[l]).astype(BF16)
    j = jnp.arange(LANES)[:, None]
    col = jnp.arange(H_A * LANES)[None, :]
    wk2 = ((j < ROPE_A) & (col % LANES == NOPE + j)).astype(BF16)
    wv = w_uv[l].reshape(KV_LORA, WA).astype(BF16)
    wuk_dec = jnp.pad(jnp.transpose(w_uk[l], (1, 2, 0)), ((0, 0), (0, LANES - NOPE), (0, 0))).astype(BF16)
    pe1 = 1.0 + nsa_pe[l]
    pew = jnp.concatenate([pe1[0], pe1[0], pe1[1], pe1[1]], axis=1)
    return dict(ng=norm_g[l][None], qng=q_norm_g[l][None], kvng=kv_norm_g[l][None], win=win, wg=wg,
                wuq=wuq, wk1=wk1, wk2=wk2, wv=wv, wuk_dec=wuk_dec, pew=pew,
                wpa=w_pa[l].astype(BF16), wpb=w_pb[l].astype(BF16), wout=w_out[l].astype(BF16))


def _rope_tables(pos):
    pos = pos.astype(F32)[:, None]
    lane = jnp.arange(LANES)[None, :]

    def table(half, active, offset):
        inv = ROPE_THETA ** (-(((lane - offset) % half).astype(F32)) / half)
        ang = pos * inv
        first = ((lane - offset) % (2 * half)) < half
        cos = jnp.where(active, jnp.cos(ang), 1.0)
        sin = jnp.where(active, jnp.where(first, -jnp.sin(ang), jnp.sin(ang)), 0.0)
        return cos, sin

    cq, sq = table(ROPE_A // 2, (lane >= NOPE) & (lane < NOPE + ROPE_A), NOPE)
    ca, sa = table(ROPE_A // 2, lane < ROPE_A, 0)
    cb, sb = table(D_B // 2, lane >= 0, 0)
    return jnp.stack([cq, sq, ca, sa, cb, sb])


def _pad_tokens(a, batch, t):
    a = a.reshape(batch, t, a.shape[-1])
    return jnp.pad(a, ((0, 0), (0, DEC_ROWS - t), (0, 0)))


def kernel(x_prompt, x_sample, c_prompt, c_sample, cache_mla, cache_nsa_cmp, cache_nsa_slc,
           state_nsa_win, page_table, norm_g, w_ada, b_ada, w_in, q_norm_g, kv_norm_g, w_uq, w_uk, w_uv,
           nsa_pe, w_pa, w_pb, w_out, final_g):
    depth = norm_g.shape[0]
    batch, seq, _ = x_prompt.shape
    dbatch, dseq, _ = x_sample.shape
    n_pool, page_rows = cache_mla.shape[1], cache_mla.shape[2]
    past_len = page_table.shape[1] * page_rows
    wbuf = state_nsa_win.shape[2]
    assert dseq <= DEC_ROWS and past_len % BLK_SLC == 0 and seq % BLK_SLC == 0 and wbuf >= dseq

    mods = _mod_call(jnp.concatenate([c_prompt, c_sample], axis=0), w_ada.astype(BF16),
                     b_ada[:, None, :])
    rope_p = _rope_tables(jnp.arange(seq, dtype=jnp.int32))
    rope_s = jnp.tile(_rope_tables(past_len + jnp.arange(dseq, dtype=jnp.int32)), (1, dbatch, 1))
    cache_cmp = cache_nsa_cmp.reshape(depth, n_pool, page_rows, KVW)
    cache_slc = cache_nsa_slc.reshape(depth, n_pool, page_rows, KVW)
    win_state = state_nsa_win.reshape(depth, dbatch, wbuf, KVW)
    fg = final_g[None]

    xp = x_prompt.reshape(batch * seq, D_MODEL)
    xs = x_sample.reshape(dbatch * dseq, D_MODEL)
    outs = [[] for _ in range(8)]
    for l in range(depth):
        w = _layer_weights(l, norm_g, w_in, q_norm_g, kv_norm_g, w_uq, w_uk, w_uv, nsa_pe, w_pa, w_pb,
                           w_out)
        final = l == depth - 1

        mod_p = mods[l, :batch][:, None, :]
        q, mla, kmla, vmla, qb, cmp_r, slc_r, win_r, gbr, kvc = _proj_call(
            xp, mod_p, rope_p, w, tokens_per_mod_row=seq, rope_period=seq, with_kvc=True)
        n_cmp = seq // BLK_CMP
        nc_pad = -(-n_cmp // LANES) * LANES
        kvc = jnp.pad(kvc.reshape(batch, n_cmp, KVW), ((0, 0), (0, nc_pad - n_cmp), (0, 0)))
        o_a, o_nsa = _attn_call(q, kmla.reshape(batch, seq, -1), vmla.reshape(batch, seq, -1), qb, gbr,
                                kvc, slc_r.reshape(batch, seq, KVW), win_r.reshape(batch, seq, KVW),
                                batch, seq)
        xp = _out_call(xp, mod_p, o_a, o_nsa, w, fg, tokens_per_mod_row=seq, final=final)
        outs[0].append(mla.reshape(batch, seq, LAT))
        outs[1].append(cmp_r.reshape(batch, seq, 2, N_KV_B, D_B))
        outs[2].append(slc_r.reshape(batch, seq, 2, N_KV_B, D_B))
        outs[3].append(win_r.reshape(batch, seq, 2, N_KV_B, D_B)[:, seq - min(WINDOW, seq):])

        mod_s = jnp.repeat(mods[l, batch:], dseq, axis=0)
        q, mla, _, _, qb, cmp_r, slc_r, win_r, gbr = _proj_call(
            xs, mod_s, rope_s, w, tokens_per_mod_row=None, rope_period=dbatch * dseq, with_kvc=False)
        o_a = _dec_mla_call(page_table, _pad_tokens(q, dbatch, dseq), w["wuk_dec"], cache_mla,
                            _pad_tokens(mla, dbatch, dseq), w["wv"], l, dseq)
        qb_p = _pad_tokens(qb, dbatch, dseq)
        o_cmp, sel = _dec_cmp_call(page_table, qb_p, cache_cmp, w["pew"], l, past_len)
        o_nsa = _dec_slc_call(page_table, qb_p, _pad_tokens(gbr, dbatch, dseq), sel, o_cmp, cache_slc,
                              _pad_tokens(slc_r, dbatch, dseq), win_state,
                              _pad_tokens(win_r, dbatch, dseq), l, past_len, dseq)
        xs = _out_call(xs, mod_s, o_a[:, :dseq].reshape(dbatch * dseq, WA),
                       o_nsa[:, :dseq].reshape(dbatch * dseq, WB), w, fg, tokens_per_mod_row=None,
                       final=final)
        outs[4].append(mla.reshape(dbatch, dseq, LAT))
        outs[5].append(cmp_r.reshape(dbatch, dseq, 2, N_KV_B, D_B))
        outs[6].append(slc_r.reshape(dbatch, dseq, 2, N_KV_B, D_B))
        outs[7].append(win_r.reshape(dbatch, dseq, 2, N_KV_B, D_B))

    new_win_sample = jnp.concatenate([state_nsa_win[:, :, dseq:], jnp.stack(outs[7])], axis=2)
    return (xp.reshape(batch, seq, D_MODEL), xs.reshape(dbatch, dseq, D_MODEL),
            jnp.stack(outs[0]), jnp.stack(outs[1]), jnp.stack(outs[2]), jnp.stack(outs[3]),
            jnp.stack(outs[4]), jnp.stack(outs[5]), jnp.stack(outs[6]), new_win_sample)
```

```python
import functools

import jax
import jax.numpy as jnp
from jax import lax
from jax.experimental import pallas as pl
from jax.experimental.pallas import tpu as pltpu

F32 = jnp.float32
BF16 = jnp.bfloat16

D_MODEL = 1024
H_A = 8
NOPE = 64
ROPE_A = 32
V_A = 64
Q_LORA = 384
KV_LORA = 256
LAT = KV_LORA + ROPE_A
MLA_SCALE = (NOPE + ROPE_A) ** -0.5
H_B = 8
N_KV_B = 2
GRP = H_B // N_KV_B
D_B = 64
BLK_CMP = 32
BLK_SLC = 64
N_SEL = 8
WINDOW = 512
FORCE_BONUS = 1e4
NSA_SCALE = D_B ** -0.5
ROPE_THETA = 10000.0
EPS = 1e-6
NEG = -1e30
LOWEST = -3e38
LOG2E = 1.4426950408889634
WA = H_A * V_A
WB = H_B * D_B
KVW = 2 * N_KV_B * D_B

LANES = 128
SUBLANES = 8
VMEM_LIMIT = 56 * 1024 * 1024

TM = 256
TQ = 256
DEC_ROWS = 8
PAGES_PER_CHUNK = 16
DEC_GROUP = 4

C_CQ = 0
C_CKV = Q_LORA
C_KR = C_CKV + KV_LORA
C_QB = C_KR + LANES
C_KVB = C_QB + WB
C_GBR = C_KVB + 3 * KVW
C_END = C_GBR + LANES


def _dot(a, b):
    return jnp.dot(a, b, preferred_element_type=F32)


def _dot_nt(a, b):
    return lax.dot_general(a, b, (((1,), (1,)), ((), ())), preferred_element_type=F32)


def _cparams(sem):
    return pltpu.CompilerParams(dimension_semantics=sem, vmem_limit_bytes=VMEM_LIMIT)


def _rms(x, g):
    return x * lax.rsqrt(jnp.mean(x * x, axis=-1, keepdims=True) + EPS) * g


def _rope_lanes(x, cos, sin_signed, half):
    lane = lax.broadcasted_iota(jnp.int32, x.shape, 1)
    first = (lane % (2 * half)) < half
    swapped = jnp.where(first, pltpu.roll(x, LANES - half, 1), pltpu.roll(x, half, 1))
    return x * cos + swapped * sin_signed


def _modulated(x, mod, ng):
    shift = mod[:, 0:D_MODEL]
    scale = mod[:, D_MODEL:2 * D_MODEL]
    return _rms(x, ng) * (1.0 + scale) + shift


def _mod_kernel(c_ref, w_ref, b_ref, o_ref):
    c = c_ref[...]
    o_ref[...] = _dot(jax.nn.silu(c).astype(BF16), w_ref[...]) + b_ref[...]


def _mod_call(c_all, w_ada, b_ada):
    depth = w_ada.shape[0]
    rows = c_all.shape[0]
    nb = 3 * D_MODEL // D_MODEL
    return pl.pallas_call(
        _mod_kernel,
        grid=(depth, nb),
        in_specs=[
            pl.BlockSpec((rows, D_MODEL), lambda l, j: (0, 0)),
            pl.BlockSpec((None, D_MODEL, D_MODEL), lambda l, j: (l, 0, j)),
            pl.BlockSpec((None, 1, D_MODEL), lambda l, j: (l, 0, j)),
        ],
        out_specs=pl.BlockSpec((None, rows, D_MODEL), lambda l, j: (l, 0, j)),
        out_shape=jax.ShapeDtypeStruct((depth, rows, 3 * D_MODEL), F32),
        compiler_params=_cparams(("arbitrary", "arbitrary")),
        name="mod",
    )(c_all, w_ada, b_ada)


def _proj_kernel(x_ref, mod_ref, rope_ref, ng_ref, qng_ref, kvng_ref, win_ref, wuq_ref, wk1_ref,
                 wk2_ref, wv_ref, pew_ref, q_out, mla_out, kmla_out, vmla_out, qb_out, cmp_out,
                 slc_out, winr_out, gbr_out, *maybe_kvc_out):
    x = x_ref[...]
    ub = _modulated(x, mod_ref[...], ng_ref[...]).astype(BF16)

    cq = _dot(ub, win_ref[:, C_CQ:C_CKV])
    q = _dot(_rms(cq, qng_ref[...]).astype(BF16), wuq_ref[...])
    cos_q, sin_q = rope_ref[0], rope_ref[1]
    for h in range(H_A):
        blk = q[:, h * LANES:(h + 1) * LANES]
        q_out[:, h * LANES:(h + 1) * LANES] = _rope_lanes(blk, cos_q, sin_q, ROPE_A // 2).astype(BF16)

    zl = _dot(ub, win_ref[:, C_CKV:C_QB])
    ckvn = _rms(zl[:, 0:KV_LORA], kvng_ref[...])
    krr = _rope_lanes(zl[:, KV_LORA:KV_LORA + LANES], rope_ref[2], rope_ref[3], ROPE_A // 2)
    mla_out[:, 0:KV_LORA] = ckvn
    mla_out[:, KV_LORA:LAT] = krr[:, 0:ROPE_A]
    ckvb = ckvn.astype(BF16)
    kmla_out[...] = (_dot(ckvb, wk1_ref[...]) + _dot(krr.astype(BF16), wk2_ref[...])).astype(BF16)
    vmla_out[...] = _dot(ckvb, wv_ref[...]).astype(BF16)

    cos_b, sin_b = rope_ref[4], rope_ref[5]
    zq = _dot(ub, win_ref[:, C_QB:C_KVB])
    for j in range(WB // LANES):
        blk = zq[:, j * LANES:(j + 1) * LANES]
        qb_out[:, j * LANES:(j + 1) * LANES] = _rope_lanes(blk, cos_b, sin_b, D_B // 2) * NSA_SCALE
    zkv = _dot(ub, win_ref[:, C_KVB:C_GBR])
    for j, ref in enumerate((cmp_out, slc_out, winr_out)):
        k = zkv[:, j * KVW:j * KVW + LANES]
        ref[:, 0:LANES] = _rope_lanes(k, cos_b, sin_b, D_B // 2)
        ref[:, LANES:KVW] = zkv[:, j * KVW + LANES:(j + 1) * KVW]
    gbr_out[...] = jax.nn.sigmoid(_dot(ub, win_ref[:, C_GBR:C_END]))

    if maybe_kvc_out:
        (kvc_out,) = maybe_kvc_out
        tm = x.shape[0]
        rows = cmp_out[...].reshape(tm // BLK_CMP, BLK_CMP, KVW) * pew_ref[...][None]
        kvc_out[...] = jnp.sum(rows, axis=1) * (1.0 / BLK_CMP)


def _proj_call(x, mod, rope_tab, w, *, tokens_per_mod_row, rope_period, with_kvc):
    n = x.shape[0]
    tm = min(TM, n)
    assert n % tm == 0 and rope_period % tm == 0
    nrope = rope_period // tm
    if tokens_per_mod_row is None:
        mod_spec = pl.BlockSpec((tm, 3 * D_MODEL), lambda i: (i, 0))
    else:
        assert tokens_per_mod_row % tm == 0
        per = tokens_per_mod_row // tm
        mod_spec = pl.BlockSpec((None, 1, 3 * D_MODEL), lambda i: (i // per, 0, 0))
    full = lambda a: pl.BlockSpec(a.shape, lambda i: (0,) * a.ndim)
    row = lambda width: pl.BlockSpec((tm, width), lambda i: (i, 0))
    out_specs = [row(H_A * LANES), row(LAT), row(H_A * LANES), row(WA), row(WB), row(KVW), row(KVW),
                 row(KVW), row(LANES)]
    out_shape = [jax.ShapeDtypeStruct((n, H_A * LANES), BF16), jax.ShapeDtypeStruct((n, LAT), F32),
                 jax.ShapeDtypeStruct((n, H_A * LANES), BF16), jax.ShapeDtypeStruct((n, WA), BF16),
                 jax.ShapeDtypeStruct((n, WB), F32), jax.ShapeDtypeStruct((n, KVW), F32),
                 jax.ShapeDtypeStruct((n, KVW), F32), jax.ShapeDtypeStruct((n, KVW), F32),
                 jax.ShapeDtypeStruct((n, LANES), F32)]
    if with_kvc:
        out_specs.append(pl.BlockSpec((tm // BLK_CMP, KVW), lambda i: (i, 0)))
        out_shape.append(jax.ShapeDtypeStruct((n // BLK_CMP, KVW), F32))
    weights = (w["ng"], w["qng"], w["kvng"], w["win"], w["wuq"], w["wk1"], w["wk2"], w["wv"], w["pew"])
    return pl.pallas_call(
        _proj_kernel,
        grid=(n // tm,),
        in_specs=[row(D_MODEL), mod_spec,
                  pl.BlockSpec((6, tm, LANES), lambda i: (0, i % nrope, 0))] + [full(a) for a in weights],
        out_specs=out_specs,
        out_shape=out_shape,
        compiler_params=_cparams(("arbitrary",)),
        name="proj",
    )(x, mod, rope_tab, *weights)


def _flash_init(m_ref, l_ref, acc_ref):
    m_ref[...] = jnp.full(m_ref.shape, NEG, F32)
    l_ref[...] = jnp.zeros(l_ref.shape, F32)
    acc_ref[...] = jnp.zeros(acc_ref.shape, F32)


def _lane_tile(a, width):
    reps = width // a.shape[-1]
    return a if reps == 1 else jnp.concatenate([a] * reps, axis=-1)


def _flash_update(s, mask, v, m_ref, l_ref, acc_ref, idx, scale=1.0, v_is_transposed=False):
    c = scale * LOG2E
    if mask is not None:
        s = jnp.where(mask, s, NEG)
    m_prev = m_ref[idx]
    l_prev = l_ref[idx]
    m_next = jnp.maximum(m_prev, jnp.max(s, axis=-1, keepdims=True))
    p = jnp.exp2((s - _lane_tile(m_next, s.shape[-1])) * c)
    if mask is not None:
        p = jnp.where(mask, p, 0.0)
    alpha = jnp.exp2((m_prev - m_next) * c)
    l_ref[idx] = alpha * l_prev + jnp.sum(p, axis=-1, keepdims=True)
    pv = _dot_nt(p.astype(BF16), v) if v_is_transposed else _dot(p.astype(BF16), v)
    acc_ref[idx] = acc_ref[idx] * _lane_tile(alpha, acc_ref.shape[-1]) + pv
    m_ref[idx] = m_next


def _flash_result(l_ref, acc_ref, idx):
    l = l_ref[idx]
    l = jnp.where(l == 0.0, 1.0, l)
    return acc_ref[idx] / _lane_tile(l, acc_ref.shape[-1])


def _group_queries(qb, g, rows):
    lane = lax.broadcasted_iota(jnp.int32, (rows, LANES), 1)
    keep = (lane // D_B) == g
    parts = []
    for r in range(GRP):
        h = g * GRP + r
        blk = qb[:, (h // 2) * LANES:(h // 2 + 1) * LANES]
        if h % 2 != g:
            blk = pltpu.roll(blk, D_B, 1)
        parts.append(jnp.where(keep, blk, 0.0))
    return jnp.concatenate(parts, axis=0).astype(BF16)


def _group_gates(gbr, g, j, rows):
    parts = []
    for r in range(GRP):
        c = 3 * (g * GRP + r) + j
        parts.append(jnp.broadcast_to(gbr[:, c:c + 1], (rows, LANES)))
    return jnp.concatenate(parts, axis=0)


def _ungroup(o_groups, rows):
    lane = lax.broadcasted_iota(jnp.int32, (rows, LANES), 1)
    blocks = []
    for j in range(WB // LANES):
        g = (2 * j) // GRP
        r0 = (2 * j) % GRP
        p0 = o_groups[g][r0 * rows:(r0 + 1) * rows]
        p1 = o_groups[g][(r0 + 1) * rows:(r0 + 2) * rows]
        if g == 1:
            p0 = pltpu.roll(p0, D_B, 1)
        else:
            p1 = pltpu.roll(p1, D_B, 1)
        blocks.append(jnp.where(lane < D_B, p0, p1))
    return blocks


def _compressed_attention(qg, kvc, qpos, n_cmp):
    nc = kvc.shape[0]
    s = _dot_nt(qg, kvc[:, 0:LANES].astype(BF16))
    c = lax.broadcasted_iota(jnp.int32, s.shape, 1)
    ok = ((c + 1) * BLK_CMP - 1 <= qpos) & (c < n_cmp)
    s = jnp.where(ok, s, NEG)
    p = jnp.where(ok, jnp.exp(s - jnp.max(s, axis=-1, keepdims=True)), 0.0)
    den = jnp.sum(p, axis=-1, keepdims=True)
    p = p / jnp.where(den == 0.0, 1.0, den)
    return p, _dot(p.astype(BF16), kvc[:, LANES:KVW].astype(BF16))


def _select_blocks(imp_cmp, cur, n_blocks):
    nc = imp_cmp.shape[-1]
    lane = lax.broadcasted_iota(jnp.int32, imp_cmp.shape, 1)
    pair = imp_cmp + pltpu.roll(imp_cmp, nc - 1, 1)
    blk = lane // 2
    forced = (blk == 0) | (blk == cur - 1)
    score = jnp.where(blk <= cur, pair + jnp.where(forced, FORCE_BONUS, 0.0), NEG)
    cand = ((lane % 2) == 0) & (blk < n_blocks) & (blk != cur)
    score = jnp.where(cand, score, LOWEST)
    lanef = lane.astype(F32)
    sel = jnp.zeros(imp_cmp.shape, F32)
    for _ in range(N_SEL - 1):
        top = jnp.max(score, axis=-1, keepdims=True)
        first = jnp.min(jnp.where(score == top, lanef, float(nc)), axis=-1, keepdims=True)
        hit = lanef == first
        sel = jnp.where(hit, 1.0, sel)
        score = jnp.where(hit, LOWEST, score)
    return sel


def _selection_matrix(nc, k0, nkeys):
    i = lax.broadcasted_iota(jnp.int32, (nc, nkeys), 0)
    k = lax.broadcasted_iota(jnp.int32, (nc, nkeys), 1) + k0
    return jnp.where(((i % 2) == 0) & ((k // BLK_SLC) == (i // 2)), 1.0, 0.0).astype(BF16)


def _expand_selection(sel, sel_matrix):
    return _dot(sel.astype(BF16), sel_matrix)


def _attn_kernel(q_ref, kmla_ref, vmla_ref, qb_ref, gbr_ref, kvc_ref, slc_ref, win_ref,
                 oa_ref, onsa_ref, m_a, l_a, acc_a, qg_s, m_s, l_s, acc_s, m_w, l_w, acc_w,
                 *, seq_len):
    tq = q_ref.shape[0]
    bk = tq
    qi = pl.program_id(1)
    q0 = qi * tq
    qpos = q0 + lax.broadcasted_iota(jnp.int32, (tq, 1), 0)
    qpos4 = jnp.concatenate([qpos] * GRP, axis=0)
    kiota = lax.broadcasted_iota(jnp.int32, (1, bk), 1)

    _flash_init(m_a, l_a, acc_a)

    def mla_body(kb, carry, diagonal=False):
        k0 = pl.multiple_of(kb * bk, bk)
        mask = (k0 + kiota) <= qpos if diagonal else None
        for h in range(H_A):
            s = _dot_nt(q_ref[:, h * LANES:(h + 1) * LANES],
                        kmla_ref[pl.ds(k0, bk), h * LANES:(h + 1) * LANES])
            v = vmla_ref[pl.ds(k0, bk), (h // 2) * LANES:(h // 2 + 1) * LANES]
            _flash_update(s, mask, v, m_a, l_a, acc_a, h, scale=MLA_SCALE)
        return carry

    lax.fori_loop(0, qi, mla_body, 0)
    mla_body(qi, 0, diagonal=True)
    lane = lax.broadcasted_iota(jnp.int32, (tq, LANES), 1)
    for j in range(WA // LANES):
        oa_ref[:, j * LANES:(j + 1) * LANES] = jnp.where(
            lane < V_A, _flash_result(l_a, acc_a, 2 * j), _flash_result(l_a, acc_a, 2 * j + 1))

    qb = qb_ref[...]
    gbr = gbr_ref[...]
    kvc = kvc_ref[...]
    n_cmp = seq_len // BLK_CMP
    n_slc = -(-seq_len // BLK_SLC)
    cur = qpos // BLK_SLC
    o_cmp = []
    sels = []
    for g in range(N_KV_B):
        qg = _group_queries(qb, g, tq)
        qg_s[g] = qg
        p, o = _compressed_attention(qg, kvc, qpos4, n_cmp)
        o_cmp.append(o)
        imp = p[0:tq]
        for r in range(1, GRP):
            imp = imp + p[r * tq:(r + 1) * tq]
        sel = _select_blocks(imp, cur, n_slc)
        clane = lax.broadcasted_iota(jnp.int32, sel.shape, 1)
        sels.append(jnp.where(clane == 2 * cur, 1.0, sel))

    _flash_init(m_s, l_s, acc_s)

    def slc_body(kb, carry, diagonal=False):
        k0 = pl.multiple_of(kb * bk, bk)
        kblk = slc_ref[pl.ds(k0, bk), 0:LANES].astype(BF16)
        vblk = slc_ref[pl.ds(k0, bk), LANES:KVW].astype(BF16)
        sel_matrix = _selection_matrix(kvc.shape[0], k0, bk)
        for g in range(N_KV_B):
            chosen = _expand_selection(sels[g], sel_matrix)
            mask = jnp.concatenate([chosen] * GRP, axis=0) > 0.5
            if diagonal:
                mask = mask & ((k0 + kiota) <= qpos4)
            s = _dot_nt(qg_s[g], kblk)
            _flash_update(s, mask, vblk, m_s, l_s, acc_s, g)
        return carry

    lax.fori_loop(0, qi, slc_body, 0)
    slc_body(qi, 0, diagonal=True)

    _flash_init(m_w, l_w, acc_w)

    def win_body(kb, carry):
        k0 = pl.multiple_of(kb * bk, bk)
        kpos = k0 + kiota
        mask = (kpos <= qpos4) & (kpos > qpos4 - WINDOW)
        kblk = win_ref[pl.ds(k0, bk), 0:LANES].astype(BF16)
        vblk = win_ref[pl.ds(k0, bk), LANES:KVW].astype(BF16)
        for g in range(N_KV_B):
            s = _dot_nt(qg_s[g], kblk)
            _flash_update(s, mask, vblk, m_w, l_w, acc_w, g)
        return carry

    lo = jnp.maximum(q0 - (WINDOW - 1), 0) // bk
    lax.fori_loop(lo, qi + 1, win_body, 0)

    o_groups = []
    for g in range(N_KV_B):
        o_groups.append(o_cmp[g] * _group_gates(gbr, g, 0, tq)
                        + _flash_result(l_s, acc_s, g) * _group_gates(gbr, g, 1, tq)
                        + _flash_result(l_w, acc_w, g) * _group_gates(gbr, g, 2, tq))
    for j, blk in enumerate(_ungroup(o_groups, tq)):
        onsa_ref[:, j * LANES:(j + 1) * LANES] = blk


def _attn_call(q, kmla, vmla, qb, gbr, kvc, slc, win, batch, seq_len):
    tq = min(TQ, seq_len)
    nq = seq_len // tq
    n = batch * seq_len
    nc_pad = kvc.shape[1]
    tile = lambda width: pl.BlockSpec((tq, width), lambda b, i: (b * nq + i, 0))
    whole = lambda rows, width: pl.BlockSpec((None, rows, width), lambda b, i: (b, 0, 0))
    stat = lambda heads, rows: pltpu.VMEM((heads, rows, LANES), F32)
    return pl.pallas_call(
        functools.partial(_attn_kernel, seq_len=seq_len),
        grid=(batch, nq),
        in_specs=[tile(H_A * LANES), whole(seq_len, H_A * LANES), whole(seq_len, WA), tile(WB),
                  tile(LANES), whole(nc_pad, KVW), whole(seq_len, KVW), whole(seq_len, KVW)],
        out_specs=[tile(WA), tile(WB)],
        out_shape=[jax.ShapeDtypeStruct((n, WA), F32), jax.ShapeDtypeStruct((n, WB), F32)],
        scratch_shapes=[stat(H_A, tq), stat(H_A, tq), stat(H_A, tq),
                        pltpu.VMEM((N_KV_B, GRP * tq, LANES), BF16),
                        stat(N_KV_B, GRP * tq), stat(N_KV_B, GRP * tq), stat(N_KV_B, GRP * tq),
                        stat(N_KV_B, GRP * tq), stat(N_KV_B, GRP * tq), stat(N_KV_B, GRP * tq)],
        compiler_params=_cparams(("arbitrary", "arbitrary")),
        name="attn",
    )(q, kmla, vmla, qb, gbr, kvc, slc, win)


def _page_copy(pt_ref, cache_ref, buf, sem, layer, bg, chunk, slot, e, i, pages_per_chunk):
    group, page_rows = buf.shape[1], cache_ref.shape[3]
    page = pt_ref[bg * group + e, chunk * pages_per_chunk + i]
    return pltpu.make_async_copy(cache_ref.at[layer, page],
                                 buf.at[slot, e, :, pl.ds(i * page_rows, page_rows)], sem.at[slot])


def _stream_chunk(pt_ref, cache_ref, buf, sem, layer, pages_per_chunk):
    bg = pl.program_id(0)
    c = pl.program_id(1)
    nbg = pl.num_programs(0)
    ncnk = pl.num_programs(1)
    group = buf.shape[1]
    step = bg * ncnk + c
    slot = step % 2

    def copies(bb, cc, sl):
        return [_page_copy(pt_ref, cache_ref, buf, sem, layer, bb, cc, sl, e, i, pages_per_chunk)
                for e in range(group) for i in range(pages_per_chunk)]

    @pl.when(step == 0)
    def _():
        for cp in copies(bg, c, slot):
            cp.start()

    @pl.when(step + 1 < nbg * ncnk)
    def _():
        last = c == ncnk - 1
        for cp in copies(jnp.where(last, bg + 1, bg), jnp.where(last, 0, c + 1), 1 - slot):
            cp.start()

    for cp in copies(bg, c, slot):
        cp.wait()
    return slot


def _dec_group(batch):
    return next(g for g in (DEC_GROUP, 2, 1) if batch % g == 0)


def _new_row_mask(rows, n_new):
    j = lax.broadcasted_iota(jnp.int32, (rows, LANES), 1)
    t = lax.broadcasted_iota(jnp.int32, (rows, LANES), 0) % DEC_ROWS
    return (j <= t) & (j < n_new)


def _pad_rows(a, rows):
    return jnp.concatenate([a, jnp.zeros((rows - a.shape[0], a.shape[1]), a.dtype)], axis=0)


def _dec_mla_kernel(pt_ref, q_ref, wuk_ref, cache_ref, new_ref, wuv_ref, o_ref,
                    buf, sem, qlat, qrope, m_r, l_r, acc_r, *, layer, pages_per_chunk, n_new):
    c = pl.program_id(1)
    slot = _stream_chunk(pt_ref, cache_ref, buf, sem, layer, pages_per_chunk)
    group = buf.shape[1]
    rows = H_A * DEC_ROWS

    @pl.when(c == 0)
    def _():
        for e in range(group):
            for h in range(H_A):
                qh = q_ref[e, :, h * LANES:(h + 1) * LANES]
                qlat[e, h * DEC_ROWS:(h + 1) * DEC_ROWS] = _dot(qh, wuk_ref[h])
                qrope[e, h * DEC_ROWS:(h + 1) * DEC_ROWS] = pltpu.roll(qh.astype(F32), LANES - NOPE, 1)
        _flash_init(m_r, l_r, acc_r)

    ql = [qlat[e].astype(BF16) for e in range(group)]
    qr = [qrope[e, :, 0:ROPE_A].astype(BF16) for e in range(group)]

    for e in range(group):
        lat_t = buf[slot, e].astype(BF16)
        s = _dot(ql[e], lat_t[0:KV_LORA]) + _dot(qr[e], lat_t[KV_LORA:LAT])
        _flash_update(s, None, lat_t[0:KV_LORA], m_r, l_r, acc_r, e, scale=MLA_SCALE, v_is_transposed=True)

    @pl.when(c == pl.num_programs(1) - 1)
    def _():
        for e in range(group):
            latn = _pad_rows(new_ref[e], LANES).astype(BF16)
            s_new = _dot_nt(ql[e], latn[:, 0:KV_LORA]) + _dot_nt(qr[e], latn[:, KV_LORA:LAT])
            _flash_update(s_new, _new_row_mask(rows, n_new), latn[:, 0:KV_LORA], m_r, l_r, acc_r, e,
                          scale=MLA_SCALE)
            o_lat = _flash_result(l_r, acc_r, e)
            x = _dot(o_lat.astype(BF16), wuv_ref[...]).reshape(H_A, DEC_ROWS, WA)
            head = lax.broadcasted_iota(jnp.int32, x.shape, 0)
            col = lax.broadcasted_iota(jnp.int32, x.shape, 2)
            o_ref[e] = jnp.sum(jnp.where(col // V_A == head, x, 0.0), axis=0)


def _dec_specs(group, pages_per_chunk, page_rows, width):
    return [pltpu.VMEM((2, group, width, pages_per_chunk * page_rows), F32), pltpu.SemaphoreType.DMA((2,))]


def _dec_mla_call(page_table, q, wuk, cache, new_rows, wuv, layer, n_new):
    batch, n_pages = page_table.shape
    page_rows = cache.shape[3]
    ppc = min(PAGES_PER_CHUNK, n_pages)
    assert n_pages % ppc == 0
    group = _dec_group(batch)
    rows = H_A * DEC_ROWS
    per_b = lambda r, w: pl.BlockSpec((group, r, w), lambda b, c, pt: (b, 0, 0))
    full = lambda a: pl.BlockSpec(a.shape, lambda b, c, pt: (0,) * a.ndim)
    grid_spec = pltpu.PrefetchScalarGridSpec(
        num_scalar_prefetch=1,
        grid=(batch // group, n_pages // ppc),
        in_specs=[per_b(DEC_ROWS, H_A * LANES), full(wuk), pl.BlockSpec(memory_space=pl.ANY),
                  per_b(DEC_ROWS, LAT), full(wuv)],
        out_specs=per_b(DEC_ROWS, WA),
        scratch_shapes=_dec_specs(group, ppc, page_rows, LAT) + [
            pltpu.VMEM((group, rows, KV_LORA), F32), pltpu.VMEM((group, rows, LANES), F32),
            pltpu.VMEM((group, rows, LANES), F32), pltpu.VMEM((group, rows, LANES), F32),
            pltpu.VMEM((group, rows, KV_LORA), F32)],
    )
    return pl.pallas_call(
        functools.partial(_dec_mla_kernel, layer=layer, pages_per_chunk=ppc, n_new=n_new),
        grid_spec=grid_spec,
        out_shape=jax.ShapeDtypeStruct((batch, DEC_ROWS, WA), F32),
        compiler_params=_cparams(("arbitrary", "arbitrary")),
        name="dec_mla",
    )(page_table, q, wuk, cache, new_rows, wuv)


def _dec_cmp_kernel(pt_ref, qb_ref, cache_ref, pewt_ref, pool_ref, oc_ref, sel_ref, buf, sem, kvc_s,
                    *, layer, pages_per_chunk, past_len):
    c = pl.program_id(1)
    slot = _stream_chunk(pt_ref, cache_ref, buf, sem, layer, pages_per_chunk)
    group, chunk_rows = buf.shape[1], buf.shape[3]
    blocks = chunk_rows // BLK_CMP
    n_cmp = past_len // BLK_CMP

    @pl.when(c == 0)
    def _():
        kvc_s[...] = jnp.zeros(kvc_s.shape, F32)

    pewt = _lane_tile(pewt_ref[...], chunk_rows)
    pool = pool_ref[...]
    for e in range(group):
        xw = buf[slot, e] * pewt
        hi = xw.astype(BF16)
        mid = (xw - hi.astype(F32)).astype(BF16)
        sums = _dot_nt(pool, hi) + _dot_nt(pool, mid)
        kvc_s[e, pl.ds(pl.multiple_of(c * blocks, SUBLANES), blocks)] = sums * (1.0 / BLK_CMP)

    @pl.when(c == pl.num_programs(1) - 1)
    def _():
        t = lax.broadcasted_iota(jnp.int32, (DEC_ROWS, 1), 0)
        qpos = past_len + t
        qpos4 = jnp.concatenate([qpos] * GRP, axis=0)
        imps = []
        for e in range(group):
            qb = qb_ref[e]
            kvc = kvc_s[e]
            for g in range(N_KV_B):
                p, o = _compressed_attention(_group_queries(qb, g, DEC_ROWS), kvc, qpos4, n_cmp)
                oc_ref[e, g] = o
                imp = p[0:DEC_ROWS]
                for r in range(1, GRP):
                    imp = imp + p[r * DEC_ROWS:(r + 1) * DEC_ROWS]
                imps.append(imp)
        n_sets = group * N_KV_B
        cur = jnp.concatenate([qpos // BLK_SLC] * n_sets, axis=0)
        sel = _select_blocks(jnp.concatenate(imps, axis=0), cur, n_cmp // (BLK_SLC // BLK_CMP))
        for e in range(group):
            for g in range(N_KV_B):
                i = e * N_KV_B + g
                sel_ref[e, g] = sel[i * DEC_ROWS:(i + 1) * DEC_ROWS]


def _dec_cmp_call(page_table, qb, cache, pew, layer, past_len):
    batch, n_pages = page_table.shape
    page_rows = cache.shape[3]
    ppc = min(PAGES_PER_CHUNK, n_pages)
    chunk_rows = ppc * page_rows
    assert n_pages % ppc == 0 and (chunk_rows // BLK_CMP) % SUBLANES == 0 and LANES % BLK_CMP == 0
    group = _dec_group(batch)
    nc_pad = -(-(past_len // BLK_CMP) // LANES) * LANES
    pewt = jnp.tile(pew.T, (1, LANES // BLK_CMP))
    pool = (jnp.arange(chunk_rows)[None, :] // BLK_CMP
            == jnp.arange(chunk_rows // BLK_CMP)[:, None]).astype(BF16)
    per_b = lambda r, w: pl.BlockSpec((group, r, w), lambda b, c, pt: (b, 0, 0))
    per_bg = lambda r, w: pl.BlockSpec((group, N_KV_B, r, w), lambda b, c, pt: (b, 0, 0, 0))
    grid_spec = pltpu.PrefetchScalarGridSpec(
        num_scalar_prefetch=1,
        grid=(batch // group, n_pages // ppc),
        in_specs=[per_b(DEC_ROWS, WB), pl.BlockSpec(memory_space=pl.ANY),
                  pl.BlockSpec(pewt.shape, lambda b, c, pt: (0, 0)),
                  pl.BlockSpec(pool.shape, lambda b, c, pt: (0, 0))],
        out_specs=[per_bg(GRP * DEC_ROWS, LANES), per_bg(DEC_ROWS, nc_pad)],
        scratch_shapes=_dec_specs(group, ppc, page_rows, KVW) + [pltpu.VMEM((group, nc_pad, KVW), F32)],
    )
    return pl.pallas_call(
        functools.partial(_dec_cmp_kernel, layer=layer, pages_per_chunk=ppc, past_len=past_len),
        grid_spec=grid_spec,
        out_shape=[jax.ShapeDtypeStruct((batch, N_KV_B, GRP * DEC_ROWS, LANES), F32),
                   jax.ShapeDtypeStruct((batch, N_KV_B, DEC_ROWS, nc_pad), F32)],
        compiler_params=_cparams(("arbitrary", "arbitrary")),
        name="dec_cmp",
    )(page_table, qb, cache, pewt, pool)


def _dec_slc_kernel(pt_ref, qb_ref, gbr_ref, sel_ref, oc_ref, cache_ref, slcn_ref, wins_ref, winn_ref,
                    emat_ref, o_ref, buf, sem, qall, m_s, l_s, acc_s,
                    *, layer, pages_per_chunk, past_len, n_new):
    c = pl.program_id(1)
    slot = _stream_chunk(pt_ref, cache_ref, buf, sem, layer, pages_per_chunk)
    group = buf.shape[1]
    rows = GRP * DEC_ROWS
    sets = N_KV_B * DEC_ROWS

    @pl.when(c == 0)
    def _():
        for e in range(group):
            qb = qb_ref[e]
            for g in range(N_KV_B):
                qall[e, g * rows:(g + 1) * rows] = _group_queries(qb, g, DEC_ROWS)
        _flash_init(m_s, l_s, acc_s)

    def per_head(per_token):
        parts = []
        for g in range(N_KV_B):
            parts += [per_token[g * DEC_ROWS:(g + 1) * DEC_ROWS]] * GRP
        return jnp.concatenate(parts, axis=0)

    q = [qall[e] for e in range(group)]
    chosen = _dot(sel_ref[...].reshape(group * sets, LANES).astype(BF16), emat_ref[...])
    for e in range(group):
        kv_t = buf[slot, e].astype(BF16)
        mask = per_head(chosen[e * sets:(e + 1) * sets]) > 0.5
        _flash_update(_dot(q[e], kv_t[0:LANES]), mask, kv_t[LANES:KVW], m_s, l_s, acc_s, e,
                      v_is_transposed=True)

    @pl.when(c == pl.num_programs(1) - 1)
    def _():
        new_mask = _new_row_mask(N_KV_B * rows, n_new)
        wbuf = wins_ref.shape[2]
        j = lax.broadcasted_iota(jnp.int32, (N_KV_B * rows, wbuf), 1)
        t = lax.broadcasted_iota(jnp.int32, (N_KV_B * rows, wbuf), 0) % DEC_ROWS
        kpos = past_len - wbuf + j
        state_mask = (kpos > past_len + t - WINDOW) & (kpos >= 0)
        o_sel = []
        for e in range(group):
            slcn = _pad_rows(slcn_ref[e], LANES).astype(BF16)
            _flash_update(_dot_nt(q[e], slcn[:, 0:LANES]), new_mask, slcn[:, LANES:KVW], m_s, l_s, acc_s, e)
            o_sel.append(_flash_result(l_s, acc_s, e))
        _flash_init(m_s, l_s, acc_s)
        for e in range(group):
            gbr = gbr_ref[e]
            wins_t = wins_ref[e].astype(BF16)
            winn = _pad_rows(winn_ref[e], LANES).astype(BF16)
            _flash_update(_dot(q[e], wins_t[0:LANES]), state_mask, wins_t[LANES:KVW], m_s, l_s, acc_s, e,
                          v_is_transposed=True)
            _flash_update(_dot_nt(q[e], winn[:, 0:LANES]), new_mask, winn[:, LANES:KVW], m_s, l_s, acc_s, e)
            o_win = _flash_result(l_s, acc_s, e)
            o_groups = []
            for g in range(N_KV_B):
                o_groups.append(oc_ref[e, g] * _group_gates(gbr, g, 0, DEC_ROWS)
                                + o_sel[e][g * rows:(g + 1) * rows] * _group_gates(gbr, g, 1, DEC_ROWS)
                                + o_win[g * rows:(g + 1) * rows] * _group_gates(gbr, g, 2, DEC_ROWS))
            for jb, blk in enumerate(_ungroup(o_groups, DEC_ROWS)):
                o_ref[e, :, jb * LANES:(jb + 1) * LANES] = blk


def _dec_slc_call(page_table, qb, gbr, sel, o_cmp, cache, slc_new, win_state, win_new, layer, past_len,
                  n_new):
    batch, n_pages = page_table.shape
    page_rows = cache.shape[3]
    ppc = min(PAGES_PER_CHUNK, n_pages)
    n_chunks = n_pages // ppc
    chunk_rows = ppc * page_rows
    wbuf = win_state.shape[3]
    rows = GRP * DEC_ROWS
    lanes_per_chunk = chunk_rows // BLK_CMP
    assert lanes_per_chunk <= LANES
    sel_c = sel[..., :n_chunks * lanes_per_chunk].reshape(batch, N_KV_B, DEC_ROWS, n_chunks, lanes_per_chunk)
    sel_c = jnp.transpose(sel_c, (0, 3, 1, 2, 4)).reshape(batch, n_chunks, N_KV_B * DEC_ROWS, lanes_per_chunk)
    sel_c = jnp.pad(sel_c, ((0, 0), (0, 0), (0, 0), (0, LANES - lanes_per_chunk)))
    i = jnp.arange(LANES)[:, None]
    k = jnp.arange(chunk_rows)[None, :]
    emat = ((i % 2 == 0) & (i < lanes_per_chunk) & (k // BLK_SLC == i // 2)).astype(BF16)
    group = _dec_group(batch)
    per_b = lambda r, w: pl.BlockSpec((group, r, w), lambda b, c, pt: (b, 0, 0))
    per_bg = lambda r, w: pl.BlockSpec((group, N_KV_B, r, w), lambda b, c, pt: (b, 0, 0, 0))
    stat = pltpu.VMEM((group, N_KV_B * rows, LANES), F32)
    grid_spec = pltpu.PrefetchScalarGridSpec(
        num_scalar_prefetch=1,
        grid=(batch // group, n_chunks),
        in_specs=[per_b(DEC_ROWS, WB), per_b(DEC_ROWS, LANES),
                  pl.BlockSpec((group, None, N_KV_B * DEC_ROWS, LANES), lambda b, c, pt: (b, c, 0, 0)),
                  per_bg(rows, LANES), pl.BlockSpec(memory_space=pl.ANY), per_b(DEC_ROWS, KVW),
                  pl.BlockSpec((None, group, KVW, wbuf), lambda b, c, pt: (layer, b, 0, 0)),
                  per_b(DEC_ROWS, KVW), pl.BlockSpec(emat.shape, lambda b, c, pt: (0, 0))],
        out_specs=per_b(DEC_ROWS, WB),
        scratch_shapes=_dec_specs(group, ppc, page_rows, KVW) + [
            pltpu.VMEM((group, N_KV_B * rows, LANES), BF16), stat, stat, stat],
    )
    return pl.pallas_call(
        functools.partial(_dec_slc_kernel, layer=layer, pages_per_chunk=ppc, past_len=past_len,
                          n_new=n_new),
        grid_spec=grid_spec,
        out_shape=jax.ShapeDtypeStruct((batch, DEC_ROWS, WB), F32),
        compiler_params=_cparams(("arbitrary", "arbitrary")),
        name="dec_slc",
    )(page_table, qb, gbr, sel_c, o_cmp, cache, slc_new, win_state, win_new, emat)


def _out_kernel(x_ref, mod_ref, oa_ref, onsa_ref, ng_ref, wg_ref, wpa_ref, wpb_ref, wout_ref, fg_ref,
                y_ref, *, final):
    x = x_ref[...]
    mod = mod_ref[...]
    ub = _modulated(x, mod, ng_ref[...]).astype(BF16)
    za = jax.nn.silu(_dot(ub, wg_ref[:, 0:WA]))
    zb = jax.nn.silu(_dot(ub, wg_ref[:, WA:WA + WB]))
    gm = jax.nn.sigmoid(_dot(ub, wg_ref[:, WA + WB:WA + WB + 2 * D_MODEL]))
    ya = _dot((oa_ref[...] * za).astype(BF16), wpa_ref[...])
    yb = _dot((onsa_ref[...] * zb).astype(BF16), wpb_ref[...])
    merged = gm[:, 0:D_MODEL] * ya + gm[:, D_MODEL:2 * D_MODEL] * yb
    xn = x + mod[:, 2 * D_MODEL:3 * D_MODEL] * _dot(merged.astype(BF16), wout_ref[...])
    y_ref[...] = _rms(xn, fg_ref[...]) if final else xn


def _out_call(x, mod, o_a, o_nsa, w, final_g, *, tokens_per_mod_row, final):
    n = x.shape[0]
    tm = min(TM, n)
    assert n % tm == 0
    if tokens_per_mod_row is None:
        mod_spec = pl.BlockSpec((tm, 3 * D_MODEL), lambda i: (i, 0))
    else:
        per = tokens_per_mod_row // tm
        mod_spec = pl.BlockSpec((None, 1, 3 * D_MODEL), lambda i: (i // per, 0, 0))
    full = lambda a: pl.BlockSpec(a.shape, lambda i: (0,) * a.ndim)
    row = lambda width: pl.BlockSpec((tm, width), lambda i: (i, 0))
    weights = (w["ng"], w["wg"], w["wpa"], w["wpb"], w["wout"], final_g)
    return pl.pallas_call(
        functools.partial(_out_kernel, final=final),
        grid=(n // tm,),
        in_specs=[row(D_MODEL), mod_spec, row(WA), row(WB)] + [full(a) for a in weights],
        out_specs=row(D_MODEL),
        out_shape=jax.ShapeDtypeStruct((n, D_MODEL), F32),
        compiler_params=_cparams(("arbitrary",)),
        name="out",
    )(x, mod, o_a, o_nsa, *weights)


def _pad_cols(a, width):
    return jnp.pad(a, ((0, 0), (0, width - a.shape[1])))


def _layer_weights(l, norm_g, w_in, q_norm_g, kv_norm_g, w_uq, w_uk, w_uv, nsa_pe, w_pa, w_pb, w_out):
    wi = w_in[l]
    o = 0
    parts = {}
    for name, n in (("cq", Q_LORA), ("ckv", KV_LORA), ("kr", ROPE_A), ("za", WA), ("qb", WB),
                    ("kvb", 3 * KVW), ("gbr", 3 * H_B), ("zb", WB), ("gm", 2 * D_MODEL)):
        parts[name] = wi[:, o:o + n]
        o += n
    win = jnp.concatenate([parts["cq"], parts["ckv"], _pad_cols(parts["kr"], LANES), parts["qb"],
                           parts["kvb"], _pad_cols(parts["gbr"], LANES)], axis=1).astype(BF16)
    wg = jnp.concatenate([parts["za"], parts["zb"], parts["gm"]], axis=1).astype(BF16)
    head_pad = lambda a: jnp.pad(a, ((0, 0), (0, 0), (0, LANES - a.shape[2]))).reshape(a.shape[0], H_A * LANES)
    wuq = head_pad(w_uq[l].reshape(Q_LORA, H_A, NOPE + ROPE_A)).astype(BF16)
    wk1 = head_pad(w_uk[l]).astype(BF16)
    j = jnp.arange(LANES)[:, None]
    col = jnp.arange(H_A * LANES)[None, :]
    wk2 = ((j < ROPE_A) & (col % LANES == NOPE + j)).astype(BF16)
    wv = w_uv[l].reshape(KV_LORA, WA).astype(BF16)
    wuk_dec = jnp.pad(jnp.transpose(w_uk[l], (1, 2, 0)), ((0, 0), (0, LANES - NOPE), (0, 0))).astype(BF16)
    pe1 = 1.0 + nsa_pe[l]
    pew = jnp.concatenate([pe1[0], pe1[0], pe1[1], pe1[1]], axis=1)
    return dict(ng=norm_g[l][None], qng=q_norm_g[l][None], kvng=kv_norm_g[l][None], win=win, wg=wg,
                wuq=wuq, wk1=wk1, wk2=wk2, wv=wv, wuk_dec=wuk_dec, pew=pew,
                wpa=w_pa[l].astype(BF16), wpb=w_pb[l].astype(BF16), wout=w_out[l].astype(BF16))


def _rope_tables(pos):
    pos = pos.astype(F32)[:, None]
    lane = jnp.arange(LANES)[None, :]

    def table(half, active, offset):
        inv = ROPE_THETA ** (-(((lane - offset) % half).astype(F32)) / half)
        ang = pos * inv
        first = ((lane - offset) % (2 * half)) < half
        cos = jnp.where(active, jnp.cos(ang), 1.0)
        sin = jnp.where(active, jnp.where(first, -jnp.sin(ang), jnp.sin(ang)), 0.0)
        return cos, sin

    cq, sq = table(ROPE_A // 2, (lane >= NOPE) & (lane < NOPE + ROPE_A), NOPE)
    ca, sa = table(ROPE_A // 2, lane < ROPE_A, 0)
    cb, sb = table(D_B // 2, lane >= 0, 0)
    return jnp.stack([cq, sq, ca, sa, cb, sb])


def _pad_tokens(a, batch, t):
    a = a.reshape(batch, t, a.shape[-1])
    return jnp.pad(a, ((0, 0), (0, DEC_ROWS - t), (0, 0)))


def kernel(x_prompt, x_sample, c_prompt, c_sample, cache_mla, cache_nsa_cmp, cache_nsa_slc,
           state_nsa_win, page_table, norm_g, w_ada, b_ada, w_in, q_norm_g, kv_norm_g, w_uq, w_uk, w_uv,
           nsa_pe, w_pa, w_pb, w_out, final_g):
    depth = norm_g.shape[0]
    batch, seq, _ = x_prompt.shape
    dbatch, dseq, _ = x_sample.shape
    n_pool, page_rows = cache_mla.shape[1], cache_mla.shape[2]
    past_len = page_table.shape[1] * page_rows
    wbuf = state_nsa_win.shape[2]
    assert dseq <= DEC_ROWS and past_len % BLK_SLC == 0 and seq % BLK_SLC == 0 and wbuf >= dseq

    mods = _mod_call(jnp.concatenate([c_prompt, c_sample], axis=0), w_ada.astype(BF16),
                     b_ada[:, None, :])
    rope_p = _rope_tables(jnp.arange(seq, dtype=jnp.int32))
    rope_s = jnp.tile(_rope_tables(past_len + jnp.arange(dseq, dtype=jnp.int32)), (1, dbatch, 1))
    cache_mla_t = jnp.swapaxes(cache_mla, 2, 3)
    feature_major = lambda a: jnp.transpose(a, (0, 1, 3, 4, 5, 2)).reshape(a.shape[0], a.shape[1], KVW, a.shape[2])
    cache_cmp = feature_major(cache_nsa_cmp)
    cache_slc = feature_major(cache_nsa_slc)
    win_state = feature_major(state_nsa_win)
    fg = final_g[None]

    xp = x_prompt.reshape(batch * seq, D_MODEL)
    xs = x_sample.reshape(dbatch * dseq, D_MODEL)
    outs = [[] for _ in range(8)]
    for l in range(depth):
        w = _layer_weights(l, norm_g, w_in, q_norm_g, kv_norm_g, w_uq, w_uk, w_uv, nsa_pe, w_pa, w_pb,
                           w_out)
        final = l == depth - 1

        mod_p = mods[l, :batch][:, None, :]
        q, mla, kmla, vmla, qb, cmp_r, slc_r, win_r, gbr, kvc = _proj_call(
            xp, mod_p, rope_p, w, tokens_per_mod_row=seq, rope_period=seq, with_kvc=True)
        n_cmp = seq // BLK_CMP
        nc_pad = -(-n_cmp // LANES) * LANES
        kvc = jnp.pad(kvc.reshape(batch, n_cmp, KVW), ((0, 0), (0, nc_pad - n_cmp), (0, 0)))
        o_a, o_nsa = _attn_call(q, kmla.reshape(batch, seq, -1), vmla.reshape(batch, seq, -1), qb, gbr,
                                kvc, slc_r.reshape(batch, seq, KVW), win_r.reshape(batch, seq, KVW),
                                batch, seq)
        xp = _out_call(xp, mod_p, o_a, o_nsa, w, fg, tokens_per_mod_row=seq, final=final)
        outs[0].append(mla.reshape(batch, seq, LAT))
        outs[1].append(cmp_r.reshape(batch, seq, 2, N_KV_B, D_B))
        outs[2].append(slc_r.reshape(batch, seq, 2, N_KV_B, D_B))
        outs[3].append(win_r.reshape(batch, seq, 2, N_KV_B, D_B)[:, seq - min(WINDOW, seq):])

        mod_s = jnp.repeat(mods[l, batch:], dseq, axis=0)
        q, mla, _, _, qb, cmp_r, slc_r, win_r, gbr = _proj_call(
            xs, mod_s, rope_s, w, tokens_per_mod_row=None, rope_period=dbatch * dseq, with_kvc=False)
        o_a = _dec_mla_call(page_table, _pad_tokens(q, dbatch, dseq), w["wuk_dec"], cache_mla_t,
                            _pad_tokens(mla, dbatch, dseq), w["wv"], l, dseq)
        qb_p = _pad_tokens(qb, dbatch, dseq)
        o_cmp, sel = _dec_cmp_call(page_table, qb_p, cache_cmp, w["pew"], l, past_len)
        o_nsa = _dec_slc_call(page_table, qb_p, _pad_tokens(gbr, dbatch, dseq), sel, o_cmp, cache_slc,
                              _pad_tokens(slc_r, dbatch, dseq), win_state,
                              _pad_tokens(win_r, dbatch, dseq), l, past_len, dseq)
        xs = _out_call(xs, mod_s, o_a[:, :dseq].reshape(dbatch * dseq, WA),
                       o_nsa[:, :dseq].reshape(dbatch * dseq, WB), w, fg, tokens_per_mod_row=None,
                       final=final)
        outs[4].append(mla.reshape(dbatch, dseq, LAT))
        outs[5].append(cmp_r.reshape(dbatch, dseq, 2, N_KV_B, D_B))
        outs[6].append(slc_r.reshape(dbatch, dseq, 2, N_KV_B, D_B))
        outs[7].append(win_r.reshape(dbatch, dseq, 2, N_KV_B, D_B))

    new_win_sample = jnp.concatenate([state_nsa_win[:, :, dseq:], jnp.stack(outs[7])], axis=2)
    return (xp.reshape(batch, seq, D_MODEL), xs.reshape(dbatch, dseq, D_MODEL),
            jnp.stack(outs[0]), jnp.stack(outs[1]), jnp.stack(outs[2]), jnp.stack(outs[3]),
            jnp.stack(outs[4]), jnp.stack(outs[5]), jnp.stack(outs[6]), new_win_sample)
```

```python
import functools

import jax
import jax.numpy as jnp
from jax import lax
from jax.experimental import pallas as pl
from jax.experimental.pallas import tpu as pltpu

F32 = jnp.float32
BF16 = jnp.bfloat16

D_MODEL = 1024
H_A = 8
NOPE = 64
ROPE_A = 32
V_A = 64
Q_LORA = 384
KV_LORA = 256
LAT = KV_LORA + ROPE_A
MLA_SCALE = (NOPE + ROPE_A) ** -0.5
H_B = 8
N_KV_B = 2
GRP = H_B // N_KV_B
D_B = 64
BLK_CMP = 32
BLK_SLC = 64
N_SEL = 8
WINDOW = 512
FORCE_BONUS = 1e4
NSA_SCALE = D_B ** -0.5
ROPE_THETA = 10000.0
EPS = 1e-6
NEG = -1e30
LOWEST = -3e38
LOG2E = 1.4426950408889634
WA = H_A * V_A
WB = H_B * D_B
KVW = 2 * N_KV_B * D_B

LANES = 128
SUBLANES = 8
VMEM_LIMIT = 56 * 1024 * 1024

TM = 512
TQ = 256
DEC_ROWS = 8
PAGES_PER_CHUNK = 16
DEC_GROUP = 4

C_CQ = 0
C_CKV = Q_LORA
C_KR = C_CKV + KV_LORA
C_QB = C_KR + LANES
C_KVB = C_QB + WB
C_GBR = C_KVB + 3 * KVW
C_END = C_GBR + LANES


def _dot(a, b):
    return jnp.dot(a, b, preferred_element_type=F32)


def _dot_nt(a, b):
    return lax.dot_general(a, b, (((1,), (1,)), ((), ())), preferred_element_type=F32)


def _cparams(sem):
    return pltpu.CompilerParams(dimension_semantics=sem, vmem_limit_bytes=VMEM_LIMIT)


def _rms(x, g):
    return x * lax.rsqrt(jnp.mean(x * x, axis=-1, keepdims=True) + EPS) * g


def _rope_lanes(x, cos, sin_signed, half):
    lane = lax.broadcasted_iota(jnp.int32, x.shape, 1)
    first = (lane % (2 * half)) < half
    swapped = jnp.where(first, pltpu.roll(x, LANES - half, 1), pltpu.roll(x, half, 1))
    return x * cos + swapped * sin_signed


def _modulated(x, mod, ng):
    shift = mod[:, 0:D_MODEL]
    scale = mod[:, D_MODEL:2 * D_MODEL]
    return _rms(x, ng) * (1.0 + scale) + shift


def _mod_kernel(c_ref, w_ref, b_ref, o_ref):
    c = c_ref[...]
    o_ref[...] = _dot(jax.nn.silu(c).astype(BF16), w_ref[...]) + b_ref[...]


def _mod_call(c_all, w_ada, b_ada):
    depth = w_ada.shape[0]
    rows = c_all.shape[0]
    nb = 3 * D_MODEL // D_MODEL
    return pl.pallas_call(
        _mod_kernel,
        grid=(depth, nb),
        in_specs=[
            pl.BlockSpec((rows, D_MODEL), lambda l, j: (0, 0)),
            pl.BlockSpec((None, D_MODEL, D_MODEL), lambda l, j: (l, 0, j)),
            pl.BlockSpec((None, 1, D_MODEL), lambda l, j: (l, 0, j)),
        ],
        out_specs=pl.BlockSpec((None, rows, D_MODEL), lambda l, j: (l, 0, j)),
        out_shape=jax.ShapeDtypeStruct((depth, rows, 3 * D_MODEL), F32),
        compiler_params=_cparams(("arbitrary", "arbitrary")),
        name="mod",
    )(c_all, w_ada, b_ada)


def _proj_kernel(x_ref, mod_ref, rope_ref, ng_ref, qng_ref, kvng_ref, win_ref, wuq_ref,
                 q_out, mla_out, qb_out, cmp_out, slc_out, winr_out, gbr_out):
    x = x_ref[...]
    ub = _modulated(x, mod_ref[...], ng_ref[...]).astype(BF16)

    cq = _dot(ub, win_ref[:, C_CQ:C_CKV])
    q = _dot(_rms(cq, qng_ref[...]).astype(BF16), wuq_ref[...])
    cos_q, sin_q = rope_ref[0], rope_ref[1]
    for h in range(H_A):
        blk = q[:, h * LANES:(h + 1) * LANES]
        q_out[:, h * LANES:(h + 1) * LANES] = _rope_lanes(blk, cos_q, sin_q, ROPE_A // 2).astype(BF16)

    zl = _dot(ub, win_ref[:, C_CKV:C_QB])
    krr = _rope_lanes(zl[:, KV_LORA:KV_LORA + LANES], rope_ref[2], rope_ref[3], ROPE_A // 2)
    mla_out[:, 0:KV_LORA] = _rms(zl[:, 0:KV_LORA], kvng_ref[...])
    mla_out[:, KV_LORA:LAT] = krr[:, 0:ROPE_A]

    cos_b, sin_b = rope_ref[4], rope_ref[5]
    zq = _dot(ub, win_ref[:, C_QB:C_KVB])
    for j in range(WB // LANES):
        blk = zq[:, j * LANES:(j + 1) * LANES]
        qb_out[:, j * LANES:(j + 1) * LANES] = _rope_lanes(blk, cos_b, sin_b, D_B // 2) * NSA_SCALE
    zkv = _dot(ub, win_ref[:, C_KVB:C_GBR])
    for j, ref in enumerate((cmp_out, slc_out, winr_out)):
        k = zkv[:, j * KVW:j * KVW + LANES]
        ref[:, 0:LANES] = _rope_lanes(k, cos_b, sin_b, D_B // 2)
        ref[:, LANES:KVW] = zkv[:, j * KVW + LANES:(j + 1) * KVW]
    gbr_out[...] = jax.nn.sigmoid(_dot(ub, win_ref[:, C_GBR:C_END]))


def _proj_call(x, mod, rope_tab, w):
    n = x.shape[0]
    tm = min(TM, n)
    assert n % tm == 0
    mod_spec = pl.BlockSpec((tm, 3 * D_MODEL), lambda i: (i, 0))
    full = lambda a: pl.BlockSpec(a.shape, lambda i: (0,) * a.ndim)
    row = lambda width: pl.BlockSpec((tm, width), lambda i: (i, 0))
    out_specs = [row(H_A * LANES), row(LAT), row(WB), row(KVW), row(KVW), row(KVW), row(LANES)]
    out_shape = [jax.ShapeDtypeStruct((n, H_A * LANES), BF16), jax.ShapeDtypeStruct((n, LAT), F32),
                 jax.ShapeDtypeStruct((n, WB), F32), jax.ShapeDtypeStruct((n, KVW), F32),
                 jax.ShapeDtypeStruct((n, KVW), F32), jax.ShapeDtypeStruct((n, KVW), F32),
                 jax.ShapeDtypeStruct((n, LANES), F32)]
    weights = (w["ng"], w["qng"], w["kvng"], w["win"], w["wuq"])
    return pl.pallas_call(
        _proj_kernel,
        grid=(n // tm,),
        in_specs=[row(D_MODEL), mod_spec,
                  pl.BlockSpec((6, tm, LANES), lambda i: (0, i, 0))] + [full(a) for a in weights],
        out_specs=out_specs,
        out_shape=out_shape,
        compiler_params=_cparams(("arbitrary",)),
        name="proj",
    )(x, mod, rope_tab, *weights)


def _proj_prompt_kernel(x_ref, mod_ref, rope_ref, ng_ref, qng_ref, kvng_ref, win_ref, wuq_ref, wk1_ref,
                        wk2_ref, wv_ref, pew_ref, q_out, mlat_out, kmla_out, vmlat_out, qb_out, cmpt_out,
                        slct_out, wint_out, slck_out, wink_out, gbrt_out, kvc_out):
    x = x_ref[...]
    tm = x.shape[0]
    ub = _modulated(x, mod_ref[...], ng_ref[...]).astype(BF16)

    cq = _dot(ub, win_ref[:, C_CQ:C_CKV])
    q = _dot(_rms(cq, qng_ref[...]).astype(BF16), wuq_ref[...])
    cos_q, sin_q = rope_ref[0], rope_ref[1]
    for h in range(H_A):
        blk = q[:, h * LANES:(h + 1) * LANES]
        q_out[:, h * LANES:(h + 1) * LANES] = _rope_lanes(blk, cos_q, sin_q, ROPE_A // 2).astype(BF16)

    zl = _dot(ub, win_ref[:, C_CKV:C_QB])
    ckvn = _rms(zl[:, 0:KV_LORA], kvng_ref[...])
    krr = _rope_lanes(zl[:, KV_LORA:KV_LORA + LANES], rope_ref[2], rope_ref[3], ROPE_A // 2)
    mlat_out[0:KV_LORA, :] = ckvn.T
    mlat_out[KV_LORA:LAT, :] = krr.T[0:ROPE_A]
    ckvb = ckvn.astype(BF16)
    kmla_out[...] = (_dot(ckvb, wk1_ref[...]) + _dot(krr.astype(BF16), wk2_ref[...])).astype(BF16)
    vmlat_out[...] = _dot(ckvb, wv_ref[...]).T.astype(BF16)

    cos_b, sin_b = rope_ref[4], rope_ref[5]
    zq = _dot(ub, win_ref[:, C_QB:C_KVB])
    for j in range(WB // LANES):
        blk = zq[:, j * LANES:(j + 1) * LANES]
        qb_out[:, j * LANES:(j + 1) * LANES] = _rope_lanes(blk, cos_b, sin_b, D_B // 2) * NSA_SCALE
    zkv = _dot(ub, win_ref[:, C_KVB:C_GBR])
    branch = []
    for j, ref in enumerate((cmpt_out, slct_out, wint_out)):
        k = _rope_lanes(zkv[:, j * KVW:j * KVW + LANES], cos_b, sin_b, D_B // 2)
        v = zkv[:, j * KVW + LANES:(j + 1) * KVW]
        ref[0:LANES, :] = k.T
        ref[LANES:KVW, :] = v.T
        branch.append((k, v))
    slck_out[...] = branch[1][0].astype(BF16)
    wink_out[...] = branch[2][0].astype(BF16)
    gbrt_out[...] = jax.nn.sigmoid(_dot(ub, win_ref[:, C_GBR:C_END])).T

    pew = pew_ref[...]
    for half, val in enumerate(branch[0]):
        rows = val.reshape(tm // BLK_CMP, BLK_CMP, LANES) * pew[:, half * LANES:(half + 1) * LANES][None]
        kvc_out[:, half * LANES:(half + 1) * LANES] = jnp.sum(rows, axis=1) * (1.0 / BLK_CMP)


def _proj_prompt_call(x, mod, rope_tab, w, batch, seq):
    n = batch * seq
    tm = min(TM, seq)
    assert seq % tm == 0
    nt = seq // tm
    full = lambda a: pl.BlockSpec(a.shape, lambda i: (0,) * a.ndim)
    row = lambda width: pl.BlockSpec((tm, width), lambda i: (i, 0))
    col = lambda feat: pl.BlockSpec((None, feat, tm), lambda i: (i // nt, 0, i % nt))
    fm = lambda feat, dt: jax.ShapeDtypeStruct((batch, feat, seq), dt)
    rm = lambda width, dt: jax.ShapeDtypeStruct((n, width), dt)
    out_specs = [row(H_A * LANES), col(LAT), row(H_A * LANES), col(WA), row(WB), col(KVW), col(KVW),
                 col(KVW), row(LANES), row(LANES), col(LANES),
                 pl.BlockSpec((tm // BLK_CMP, KVW), lambda i: (i, 0))]
    out_shape = [rm(H_A * LANES, BF16), fm(LAT, F32), rm(H_A * LANES, BF16), fm(WA, BF16), rm(WB, F32),
                 fm(KVW, F32), fm(KVW, F32), fm(KVW, F32), rm(LANES, BF16), rm(LANES, BF16),
                 fm(LANES, F32), jax.ShapeDtypeStruct((n // BLK_CMP, KVW), F32)]
    weights = (w["ng"], w["qng"], w["kvng"], w["win"], w["wuq"], w["wk1"], w["wk2"], w["wv"], w["pew"])
    return pl.pallas_call(
        _proj_prompt_kernel,
        grid=(n // tm,),
        in_specs=[row(D_MODEL), pl.BlockSpec((None, 1, 3 * D_MODEL), lambda i: (i // nt, 0, 0)),
                  pl.BlockSpec((6, tm, LANES), lambda i: (0, i % nt, 0))] + [full(a) for a in weights],
        out_specs=out_specs,
        out_shape=out_shape,
        compiler_params=_cparams(("arbitrary",)),
        name="proj_prompt",
    )(x, mod, rope_tab, *weights)


def _flash_init(m_ref, l_ref, acc_ref):
    m_ref[...] = jnp.full(m_ref.shape, NEG, F32)
    l_ref[...] = jnp.zeros(l_ref.shape, F32)
    acc_ref[...] = jnp.zeros(acc_ref.shape, F32)


def _lane_tile(a, width):
    reps = width // a.shape[-1]
    return a if reps == 1 else jnp.concatenate([a] * reps, axis=-1)


def _flash_update(s, mask, v, m_ref, l_ref, acc_ref, idx, scale=1.0, v_is_transposed=False):
    c = scale * LOG2E
    if mask is not None:
        s = jnp.where(mask, s, NEG)
    m_prev = m_ref[idx]
    l_prev = l_ref[idx]
    m_next = jnp.maximum(m_prev, jnp.max(s, axis=-1, keepdims=True))
    p = jnp.exp2((s - _lane_tile(m_next, s.shape[-1])) * c)
    if mask is not None:
        p = jnp.where(mask, p, 0.0)
    alpha = jnp.exp2((m_prev - m_next) * c)
    l_ref[idx] = alpha * l_prev + jnp.sum(p, axis=-1, keepdims=True)
    pv = _dot_nt(p.astype(BF16), v) if v_is_transposed else _dot(p.astype(BF16), v)
    acc_ref[idx] = acc_ref[idx] * _lane_tile(alpha, acc_ref.shape[-1]) + pv
    m_ref[idx] = m_next


def _flash_result(l_ref, acc_ref, idx):
    l = l_ref[idx]
    l = jnp.where(l == 0.0, 1.0, l)
    return acc_ref[idx] / _lane_tile(l, acc_ref.shape[-1])


def _group_queries(qb, g, rows):
    lane = lax.broadcasted_iota(jnp.int32, (rows, LANES), 1)
    keep = (lane // D_B) == g
    parts = []
    for r in range(GRP):
        h = g * GRP + r
        blk = qb[:, (h // 2) * LANES:(h // 2 + 1) * LANES]
        if h % 2 != g:
            blk = pltpu.roll(blk, D_B, 1)
        parts.append(jnp.where(keep, blk, 0.0))
    return jnp.concatenate(parts, axis=0).astype(BF16)


def _group_gates(gbr, g, j, rows):
    parts = []
    for r in range(GRP):
        c = 3 * (g * GRP + r) + j
        parts.append(jnp.broadcast_to(gbr[:, c:c + 1], (rows, LANES)))
    return jnp.concatenate(parts, axis=0)


def _ungroup(o_groups, rows):
    lane = lax.broadcasted_iota(jnp.int32, (rows, LANES), 1)
    blocks = []
    for j in range(WB // LANES):
        g = (2 * j) // GRP
        r0 = (2 * j) % GRP
        p0 = o_groups[g][r0 * rows:(r0 + 1) * rows]
        p1 = o_groups[g][(r0 + 1) * rows:(r0 + 2) * rows]
        if g == 1:
            p0 = pltpu.roll(p0, D_B, 1)
        else:
            p1 = pltpu.roll(p1, D_B, 1)
        blocks.append(jnp.where(lane < D_B, p0, p1))
    return blocks


def _compressed_attention(qg, kvc, qpos, n_cmp):
    nc = kvc.shape[0]
    s = _dot_nt(qg, kvc[:, 0:LANES].astype(BF16))
    c = lax.broadcasted_iota(jnp.int32, s.shape, 1)
    ok = ((c + 1) * BLK_CMP - 1 <= qpos) & (c < n_cmp)
    s = jnp.where(ok, s, NEG)
    p = jnp.where(ok, jnp.exp(s - jnp.max(s, axis=-1, keepdims=True)), 0.0)
    den = jnp.sum(p, axis=-1, keepdims=True)
    p = p / jnp.where(den == 0.0, 1.0, den)
    return p, _dot(p.astype(BF16), kvc[:, LANES:KVW].astype(BF16))


def _select_blocks(imp_cmp, cur, n_blocks):
    nc = imp_cmp.shape[-1]
    lane = lax.broadcasted_iota(jnp.int32, imp_cmp.shape, 1)
    pair = imp_cmp + pltpu.roll(imp_cmp, nc - 1, 1)
    blk = lane // 2
    forced = (blk == 0) | (blk == cur - 1)
    score = jnp.where(blk <= cur, pair + jnp.where(forced, FORCE_BONUS, 0.0), NEG)
    cand = ((lane % 2) == 0) & (blk < n_blocks) & (blk != cur)
    score = jnp.where(cand, score, LOWEST)
    lanef = lane.astype(F32)
    sel = jnp.zeros(imp_cmp.shape, F32)
    for _ in range(N_SEL - 1):
        top = jnp.max(score, axis=-1, keepdims=True)
        first = jnp.min(jnp.where(score == top, lanef, float(nc)), axis=-1, keepdims=True)
        hit = lanef == first
        sel = jnp.where(hit, 1.0, sel)
        score = jnp.where(hit, LOWEST, score)
    return sel


def _flash_t_init(m_ref, l_ref, acc_ref):
    m_ref[...] = jnp.full(m_ref.shape, 0.1 * NEG, F32)
    l_ref[...] = jnp.zeros(l_ref.shape, F32)
    acc_ref[...] = jnp.zeros(acc_ref.shape, F32)


def _flash_t_update(s_t, mask_t, v_t, m_ref, l_ref, acc_ref, scale=1.0):
    c = scale * LOG2E
    if mask_t is not None:
        s_t = jnp.where(mask_t, s_t, NEG)
    m_prev = m_ref[...]
    m_next = jnp.maximum(m_prev, jnp.max(s_t, axis=0, keepdims=True))
    p = jnp.exp2((s_t - m_next) * c)
    alpha = jnp.exp2((m_prev - m_next) * c)
    l_ref[...] = alpha * l_ref[...] + jnp.sum(p, axis=0, keepdims=True)
    acc_ref[...] = acc_ref[...] * alpha + _dot(v_t, p.astype(BF16))
    m_ref[...] = m_next


def _flash_t_result(l_ref, acc_ref):
    l = l_ref[...]
    return acc_ref[...] / jnp.where(l == 0.0, 1.0, l)


def _select_blocks_t(imp_cmp, cur, n_blocks):
    nc = imp_cmp.shape[0]
    row = lax.broadcasted_iota(jnp.int32, imp_cmp.shape, 0)
    pair = imp_cmp + pltpu.roll(imp_cmp, nc - 1, 0)
    blk = row // 2
    forced = (blk == 0) | (blk == cur - 1)
    score = jnp.where(blk <= cur, pair + jnp.where(forced, FORCE_BONUS, 0.0), NEG)
    cand = ((row % 2) == 0) & (blk < n_blocks) & (blk != cur)
    score = jnp.where(cand, score, LOWEST)
    rowf = row.astype(F32)
    sel = jnp.zeros(imp_cmp.shape, F32)
    for _ in range(N_SEL - 1):
        top = jnp.max(score, axis=0, keepdims=True)
        first = jnp.min(jnp.where(score == top, rowf, float(nc)), axis=0, keepdims=True)
        hit = rowf == first
        sel = jnp.where(hit, 1.0, sel)
        score = jnp.where(hit, LOWEST, score)
    return sel


def _attn_t_kernel(q_ref, kmla_ref, vmlat_ref, qb_ref, gbrt_ref, kvc_ref, slck_ref, slct_ref, wink_ref,
                   wint_ref, oa_ref, onsa_ref, m_a, l_a, acc_a, qg_s, oc_s, m_s, l_s, acc_s, m_w, l_w, acc_w,
                   *, seq_len):
    tq = q_ref.shape[0]
    bk = tq
    rq = GRP * tq
    qi = pl.program_id(1)
    q0 = qi * tq
    qpos = q0 + lax.broadcasted_iota(jnp.int32, (1, tq), 1)
    qpos4 = q0 + lax.broadcasted_iota(jnp.int32, (1, rq), 1) % tq
    kiota = lax.broadcasted_iota(jnp.int32, (bk, 1), 0)

    _flash_t_init(m_a, l_a, acc_a)

    def mla_body(kb, carry, diagonal=False):
        k0 = pl.multiple_of(kb * bk, bk)
        mask = (k0 + kiota) <= qpos if diagonal else None
        scores = [_dot_nt(kmla_ref[pl.ds(k0, bk), h * LANES:(h + 1) * LANES],
                          q_ref[:, h * LANES:(h + 1) * LANES]) for h in range(H_A)]
        for h in range(H_A):
            v_t = vmlat_ref[h * V_A:(h + 1) * V_A, pl.ds(k0, bk)]
            _flash_t_update(scores[h], mask, v_t, m_a.at[h], l_a.at[h], acc_a.at[h * V_A:(h + 1) * V_A],
                            scale=MLA_SCALE)
        return carry

    lax.fori_loop(0, qi, mla_body, 0)
    mla_body(qi, 0, diagonal=True)
    o_a_t = jnp.concatenate([_flash_t_result(l_a.at[h], acc_a.at[h * V_A:(h + 1) * V_A])
                             for h in range(H_A)], axis=0)
    oa_ref[...] = o_a_t.T

    qb = qb_ref[...]
    kvc = kvc_ref[...]
    kc = kvc[:, 0:LANES].astype(BF16)
    vc_t = kvc[:, LANES:KVW].T.astype(BF16)
    n_cmp = seq_len // BLK_CMP
    n_slc = -(-seq_len // BLK_SLC)
    crow = lax.broadcasted_iota(jnp.int32, (kvc.shape[0], rq), 0)
    c_ok = ((crow + 1) * BLK_CMP - 1 <= qpos4) & (crow < n_cmp)
    sels = []
    for g in range(N_KV_B):
        qg = _group_queries(qb, g, tq)
        qg_s[g] = qg
        s_t = jnp.where(c_ok, _dot_nt(kc, qg), NEG)
        p = jnp.where(c_ok, jnp.exp(s_t - jnp.max(s_t, axis=0, keepdims=True)), 0.0)
        den = jnp.sum(p, axis=0, keepdims=True)
        p = p / jnp.where(den == 0.0, 1.0, den)
        oc_s[g] = _dot(vc_t[g * D_B:(g + 1) * D_B], p.astype(BF16))
        imp = p[:, 0:tq]
        for r in range(1, GRP):
            imp = imp + p[:, r * tq:(r + 1) * tq]
        cur = qpos // BLK_SLC
        sel = _select_blocks_t(imp, cur, n_slc)
        srow = lax.broadcasted_iota(jnp.int32, sel.shape, 0)
        sels.append(jnp.where(srow == 2 * cur, 1.0, sel).astype(BF16))

    _flash_t_init(m_s, l_s, acc_s)

    def slc_body(kb, carry, diagonal=False):
        k0 = pl.multiple_of(kb * bk, bk)
        kblk = slck_ref[pl.ds(k0, bk), :]
        v_t = slct_ref[LANES:KVW, pl.ds(k0, bk)].astype(BF16)
        kk = lax.broadcasted_iota(jnp.int32, (bk, kvc.shape[0]), 0) + k0
        ci = lax.broadcasted_iota(jnp.int32, (bk, kvc.shape[0]), 1)
        in_block = jnp.where(((ci % 2) == 0) & ((kk // BLK_SLC) == (ci // 2)), 1.0, 0.0).astype(BF16)
        masks = []
        for g in range(N_KV_B):
            mask = _dot(in_block, sels[g]) > 0.5
            masks.append(mask & ((k0 + kiota) <= qpos) if diagonal else mask)
        head_update(kblk, v_t, masks, m_s, l_s, acc_s)
        return carry

    def head_update(kblk, v_t, masks, m_ref, l_ref, acc_ref):
        scores = [_dot_nt(kblk, qg_s[h // GRP, (h % GRP) * tq:(h % GRP + 1) * tq]) for h in range(H_B)]
        for h in range(H_B):
            g = h // GRP
            _flash_t_update(scores[h], masks[g], v_t[g * D_B:(g + 1) * D_B], m_ref.at[h], l_ref.at[h],
                            acc_ref.at[h * D_B:(h + 1) * D_B])

    lax.fori_loop(0, qi, slc_body, 0)
    slc_body(qi, 0, diagonal=True)

    _flash_t_init(m_w, l_w, acc_w)

    def win_body(kb, carry, diagonal=False):
        k0 = pl.multiple_of(kb * bk, bk)
        kpos = k0 + kiota
        mask = kpos <= qpos if diagonal else kpos > qpos - WINDOW
        kblk = wink_ref[pl.ds(k0, bk), :]
        v_t = wint_ref[LANES:KVW, pl.ds(k0, bk)].astype(BF16)
        head_update(kblk, v_t, [mask] * N_KV_B, m_w, l_w, acc_w)
        return carry

    lo = jnp.maximum(q0 - (WINDOW - 1), 0) // bk
    lax.fori_loop(lo, qi, win_body, 0)
    win_body(qi, 0, diagonal=True)

    pieces = []
    for h in range(H_B):
        g, r = h // GRP, h % GRP
        rows = slice(h * D_B, (h + 1) * D_B)
        pieces.append(oc_s[g, :, r * tq:(r + 1) * tq] * gbrt_ref[3 * h:3 * h + 1, :]
                      + _flash_t_result(l_s.at[h], acc_s.at[rows]) * gbrt_ref[3 * h + 1:3 * h + 2, :]
                      + _flash_t_result(l_w.at[h], acc_w.at[rows]) * gbrt_ref[3 * h + 2:3 * h + 3, :])
    onsa_ref[...] = jnp.concatenate(pieces, axis=0).T


def _attn_t_call(q, kmla, vmlat, qb, gbrt, kvc, slck, slct, wink, wint, batch, seq_len):
    tq = min(TQ, seq_len)
    assert seq_len % tq == 0 and tq <= WINDOW
    nq = seq_len // tq
    n = batch * seq_len
    nc_pad = kvc.shape[1]
    rq = GRP * tq
    tile = lambda width: pl.BlockSpec((tq, width), lambda b, i: (b * nq + i, 0))
    rows = lambda width: pl.BlockSpec((None, seq_len, width), lambda b, i: (b, 0, 0))
    cols = lambda feat: pl.BlockSpec((None, feat, seq_len), lambda b, i: (b, 0, 0))
    stat = lambda sets, r: pltpu.VMEM((sets, 1, r), F32)
    return pl.pallas_call(
        functools.partial(_attn_t_kernel, seq_len=seq_len),
        grid=(batch, nq),
        in_specs=[tile(H_A * LANES), rows(H_A * LANES), cols(WA), tile(WB),
                  pl.BlockSpec((None, LANES, tq), lambda b, i: (b, 0, i)),
                  pl.BlockSpec((None, nc_pad, KVW), lambda b, i: (b, 0, 0)),
                  rows(LANES), cols(KVW), rows(LANES), cols(KVW)],
        out_specs=[tile(WA), tile(WB)],
        out_shape=[jax.ShapeDtypeStruct((n, WA), F32), jax.ShapeDtypeStruct((n, WB), F32)],
        scratch_shapes=[stat(H_A, tq), stat(H_A, tq), pltpu.VMEM((WA, tq), F32),
                        pltpu.VMEM((N_KV_B, rq, LANES), BF16), pltpu.VMEM((N_KV_B, D_B, rq), F32),
                        stat(H_B, tq), stat(H_B, tq), pltpu.VMEM((WB, tq), F32),
                        stat(H_B, tq), stat(H_B, tq), pltpu.VMEM((WB, tq), F32)],
        compiler_params=_cparams(("arbitrary", "arbitrary")),
        name="attn",
    )(q, kmla, vmlat, qb, gbrt, kvc, slck, slct, wink, wint)


def _page_copy(pt_ref, cache_ref, buf, sem, layer, bg, chunk, slot, e, i, pages_per_chunk):
    group, page_rows = buf.shape[1], cache_ref.shape[3]
    page = pt_ref[bg * group + e, chunk * pages_per_chunk + i]
    return pltpu.make_async_copy(cache_ref.at[layer, page],
                                 buf.at[slot, e, :, pl.ds(i * page_rows, page_rows)], sem.at[slot])


def _stream_chunk(pt_ref, cache_ref, buf, sem, layer, pages_per_chunk):
    bg = pl.program_id(0)
    c = pl.program_id(1)
    nbg = pl.num_programs(0)
    ncnk = pl.num_programs(1)
    group = buf.shape[1]
    step = bg * ncnk + c
    slot = step % 2

    def copies(bb, cc, sl):
        return [_page_copy(pt_ref, cache_ref, buf, sem, layer, bb, cc, sl, e, i, pages_per_chunk)
                for e in range(group) for i in range(pages_per_chunk)]

    @pl.when(step == 0)
    def _():
        for cp in copies(bg, c, slot):
            cp.start()

    @pl.when(step + 1 < nbg * ncnk)
    def _():
        last = c == ncnk - 1
        for cp in copies(jnp.where(last, bg + 1, bg), jnp.where(last, 0, c + 1), 1 - slot):
            cp.start()

    for cp in copies(bg, c, slot):
        cp.wait()
    return slot


def _dec_group(batch):
    return next(g for g in (DEC_GROUP, 2, 1) if batch % g == 0)


def _new_row_mask(rows, n_new):
    j = lax.broadcasted_iota(jnp.int32, (rows, LANES), 1)
    t = lax.broadcasted_iota(jnp.int32, (rows, LANES), 0) % DEC_ROWS
    return (j <= t) & (j < n_new)


def _pad_rows(a, rows):
    return jnp.concatenate([a, jnp.zeros((rows - a.shape[0], a.shape[1]), a.dtype)], axis=0)


def _dec_mla_kernel(pt_ref, q_ref, wuk_ref, cache_ref, new_ref, wuv_ref, o_ref,
                    buf, sem, qlat, qrope, m_r, l_r, acc_r, *, layer, pages_per_chunk, n_new):
    c = pl.program_id(1)
    slot = _stream_chunk(pt_ref, cache_ref, buf, sem, layer, pages_per_chunk)
    group = buf.shape[1]
    rows = H_A * DEC_ROWS

    @pl.when(c == 0)
    def _():
        for e in range(group):
            for h in range(H_A):
                qh = q_ref[e, :, h * LANES:(h + 1) * LANES]
                qlat[e, h * DEC_ROWS:(h + 1) * DEC_ROWS] = _dot(qh, wuk_ref[h])
                qrope[e, h * DEC_ROWS:(h + 1) * DEC_ROWS] = pltpu.roll(qh.astype(F32), LANES - NOPE, 1)
        _flash_init(m_r, l_r, acc_r)

    ql = [qlat[e].astype(BF16) for e in range(group)]
    qr = [qrope[e, :, 0:ROPE_A].astype(BF16) for e in range(group)]

    lat_t = [buf[slot, e].astype(BF16) for e in range(group)]
    scores = [_dot(ql[e], lat_t[e][0:KV_LORA]) + _dot(qr[e], lat_t[e][KV_LORA:LAT]) for e in range(group)]
    for e in range(group):
        _flash_update(scores[e], None, lat_t[e][0:KV_LORA], m_r, l_r, acc_r, e, scale=MLA_SCALE,
                      v_is_transposed=True)

    @pl.when(c == pl.num_programs(1) - 1)
    def _():
        for e in range(group):
            latn = _pad_rows(new_ref[e], LANES).astype(BF16)
            s_new = _dot_nt(ql[e], latn[:, 0:KV_LORA]) + _dot_nt(qr[e], latn[:, KV_LORA:LAT])
            _flash_update(s_new, _new_row_mask(rows, n_new), latn[:, 0:KV_LORA], m_r, l_r, acc_r, e,
                          scale=MLA_SCALE)
            o_lat = _flash_result(l_r, acc_r, e)
            x = _dot(o_lat.astype(BF16), wuv_ref[...]).reshape(H_A, DEC_ROWS, WA)
            head = lax.broadcasted_iota(jnp.int32, x.shape, 0)
            col = lax.broadcasted_iota(jnp.int32, x.shape, 2)
            o_ref[e] = jnp.sum(jnp.where(col // V_A == head, x, 0.0), axis=0)


def _dec_specs(group, pages_per_chunk, page_rows, width):
    return [pltpu.VMEM((2, group, width, pages_per_chunk * page_rows), F32), pltpu.SemaphoreType.DMA((2,))]


def _dec_mla_call(page_table, q, wuk, cache, new_rows, wuv, layer, n_new):
    batch, n_pages = page_table.shape
    page_rows = cache.shape[3]
    ppc = min(PAGES_PER_CHUNK, n_pages)
    assert n_pages % ppc == 0
    group = _dec_group(batch)
    rows = H_A * DEC_ROWS
    per_b = lambda r, w: pl.BlockSpec((group, r, w), lambda b, c, pt: (b, 0, 0))
    full = lambda a: pl.BlockSpec(a.shape, lambda b, c, pt: (0,) * a.ndim)
    grid_spec = pltpu.PrefetchScalarGridSpec(
        num_scalar_prefetch=1,
        grid=(batch // group, n_pages // ppc),
        in_specs=[per_b(DEC_ROWS, H_A * LANES), full(wuk), pl.BlockSpec(memory_space=pl.ANY),
                  per_b(DEC_ROWS, LAT), full(wuv)],
        out_specs=per_b(DEC_ROWS, WA),
        scratch_shapes=_dec_specs(group, ppc, page_rows, LAT) + [
            pltpu.VMEM((group, rows, KV_LORA), F32), pltpu.VMEM((group, rows, LANES), F32),
            pltpu.VMEM((group, rows, LANES), F32), pltpu.VMEM((group, rows, LANES), F32),
            pltpu.VMEM((group, rows, KV_LORA), F32)],
    )
    return pl.pallas_call(
        functools.partial(_dec_mla_kernel, layer=layer, pages_per_chunk=ppc, n_new=n_new),
        grid_spec=grid_spec,
        out_shape=jax.ShapeDtypeStruct((batch, DEC_ROWS, WA), F32),
        compiler_params=_cparams(("arbitrary", "arbitrary")),
        name="dec_mla",
    )(page_table, q, wuk, cache, new_rows, wuv)


def _dec_cmp_kernel(pt_ref, qb_ref, cache_ref, pewt_ref, pool_ref, oc_ref, sel_ref, buf, sem, kvc_s,
                    *, layer, pages_per_chunk, past_len):
    c = pl.program_id(1)
    slot = _stream_chunk(pt_ref, cache_ref, buf, sem, layer, pages_per_chunk)
    group, chunk_rows = buf.shape[1], buf.shape[3]
    blocks = chunk_rows // BLK_CMP
    n_cmp = past_len // BLK_CMP

    @pl.when(c == 0)
    def _():
        kvc_s[...] = jnp.zeros(kvc_s.shape, F32)

    pewt = _lane_tile(pewt_ref[...], chunk_rows)
    pool = pool_ref[...]
    for e in range(group):
        xw = buf[slot, e] * pewt
        hi = xw.astype(BF16)
        mid = (xw - hi.astype(F32)).astype(BF16)
        sums = _dot_nt(pool, hi) + _dot_nt(pool, mid)
        kvc_s[e, pl.ds(pl.multiple_of(c * blocks, SUBLANES), blocks)] = sums * (1.0 / BLK_CMP)

    @pl.when(c == pl.num_programs(1) - 1)
    def _():
        t = lax.broadcasted_iota(jnp.int32, (DEC_ROWS, 1), 0)
        qpos = past_len + t
        qpos4 = jnp.concatenate([qpos] * GRP, axis=0)
        imps = []
        for e in range(group):
            qb = qb_ref[e]
            kvc = kvc_s[e]
            for g in range(N_KV_B):
                p, o = _compressed_attention(_group_queries(qb, g, DEC_ROWS), kvc, qpos4, n_cmp)
                oc_ref[e, g] = o
                imp = p[0:DEC_ROWS]
                for r in range(1, GRP):
                    imp = imp + p[r * DEC_ROWS:(r + 1) * DEC_ROWS]
                imps.append(imp)
        n_sets = group * N_KV_B
        cur = jnp.concatenate([qpos // BLK_SLC] * n_sets, axis=0)
        sel = _select_blocks(jnp.concatenate(imps, axis=0), cur, n_cmp // (BLK_SLC // BLK_CMP))
        for e in range(group):
            for g in range(N_KV_B):
                i = e * N_KV_B + g
                sel_ref[e, g] = sel[i * DEC_ROWS:(i + 1) * DEC_ROWS]


def _dec_cmp_call(page_table, qb, cache, pew, layer, past_len):
    batch, n_pages = page_table.shape
    page_rows = cache.shape[3]
    ppc = min(PAGES_PER_CHUNK, n_pages)
    chunk_rows = ppc * page_rows
    assert n_pages % ppc == 0 and (chunk_rows // BLK_CMP) % SUBLANES == 0 and LANES % BLK_CMP == 0
    group = _dec_group(batch)
    nc_pad = -(-(past_len // BLK_CMP) // LANES) * LANES
    pewt = jnp.tile(pew.T, (1, LANES // BLK_CMP))
    pool = (jnp.arange(chunk_rows)[None, :] // BLK_CMP
            == jnp.arange(chunk_rows // BLK_CMP)[:, None]).astype(BF16)
    per_b = lambda r, w: pl.BlockSpec((group, r, w), lambda b, c, pt: (b, 0, 0))
    per_bg = lambda r, w: pl.BlockSpec((group, N_KV_B, r, w), lambda b, c, pt: (b, 0, 0, 0))
    grid_spec = pltpu.PrefetchScalarGridSpec(
        num_scalar_prefetch=1,
        grid=(batch // group, n_pages // ppc),
        in_specs=[per_b(DEC_ROWS, WB), pl.BlockSpec(memory_space=pl.ANY),
                  pl.BlockSpec(pewt.shape, lambda b, c, pt: (0, 0)),
                  pl.BlockSpec(pool.shape, lambda b, c, pt: (0, 0))],
        out_specs=[per_bg(GRP * DEC_ROWS, LANES), per_bg(DEC_ROWS, nc_pad)],
        scratch_shapes=_dec_specs(group, ppc, page_rows, KVW) + [pltpu.VMEM((group, nc_pad, KVW), F32)],
    )
    return pl.pallas_call(
        functools.partial(_dec_cmp_kernel, layer=layer, pages_per_chunk=ppc, past_len=past_len),
        grid_spec=grid_spec,
        out_shape=[jax.ShapeDtypeStruct((batch, N_KV_B, GRP * DEC_ROWS, LANES), F32),
                   jax.ShapeDtypeStruct((batch, N_KV_B, DEC_ROWS, nc_pad), F32)],
        compiler_params=_cparams(("arbitrary", "arbitrary")),
        name="dec_cmp",
    )(page_table, qb, cache, pewt, pool)


def _dec_slc_kernel(pt_ref, qb_ref, gbr_ref, sel_ref, oc_ref, cache_ref, slcn_ref, wins_ref, winn_ref,
                    emat_ref, o_ref, nws_ref, buf, sem, qall, m_s, l_s, acc_s,
                    *, layer, pages_per_chunk, past_len, n_new):
    c = pl.program_id(1)
    slot = _stream_chunk(pt_ref, cache_ref, buf, sem, layer, pages_per_chunk)
    group = buf.shape[1]
    rows = GRP * DEC_ROWS
    sets = N_KV_B * DEC_ROWS

    @pl.when(c == 0)
    def _():
        for e in range(group):
            qb = qb_ref[e]
            for g in range(N_KV_B):
                qall[e, g * rows:(g + 1) * rows] = _group_queries(qb, g, DEC_ROWS)
        _flash_init(m_s, l_s, acc_s)

    def per_head(per_token):
        parts = []
        for g in range(N_KV_B):
            parts += [per_token[g * DEC_ROWS:(g + 1) * DEC_ROWS]] * GRP
        return jnp.concatenate(parts, axis=0)

    q = [qall[e] for e in range(group)]
    chosen = _dot(sel_ref[...].reshape(group * sets, LANES).astype(BF16), emat_ref[...])
    kv_t = [buf[slot, e].astype(BF16) for e in range(group)]
    scores = [_dot(q[e], kv_t[e][0:LANES]) for e in range(group)]
    for e in range(group):
        mask = per_head(chosen[e * sets:(e + 1) * sets]) > 0.5
        _flash_update(scores[e], mask, kv_t[e][LANES:KVW], m_s, l_s, acc_s, e, v_is_transposed=True)

    @pl.when(c == pl.num_programs(1) - 1)
    def _():
        new_mask = _new_row_mask(N_KV_B * rows, n_new)
        wbuf = wins_ref.shape[2]
        j = lax.broadcasted_iota(jnp.int32, (N_KV_B * rows, wbuf), 1)
        t = lax.broadcasted_iota(jnp.int32, (N_KV_B * rows, wbuf), 0) % DEC_ROWS
        kpos = past_len - wbuf + j
        state_mask = (kpos > past_len + t - WINDOW) & (kpos >= 0)
        o_sel = []
        for e in range(group):
            slcn = _pad_rows(slcn_ref[e], LANES).astype(BF16)
            _flash_update(_dot_nt(q[e], slcn[:, 0:LANES]), new_mask, slcn[:, LANES:KVW], m_s, l_s, acc_s, e)
            o_sel.append(_flash_result(l_s, acc_s, e))
        _flash_init(m_s, l_s, acc_s)
        for e in range(group):
            gbr = gbr_ref[e]
            wins_t = wins_ref[e].astype(BF16)
            winn = _pad_rows(winn_ref[e], LANES).astype(BF16)
            _flash_update(_dot(q[e], wins_t[0:LANES]), state_mask, wins_t[LANES:KVW], m_s, l_s, acc_s, e,
                          v_is_transposed=True)
            _flash_update(_dot_nt(q[e], winn[:, 0:LANES]), new_mask, winn[:, LANES:KVW], m_s, l_s, acc_s, e)
            o_win = _flash_result(l_s, acc_s, e)
            o_groups = []
            for g in range(N_KV_B):
                o_groups.append(oc_ref[e, g] * _group_gates(gbr, g, 0, DEC_ROWS)
                                + o_sel[e][g * rows:(g + 1) * rows] * _group_gates(gbr, g, 1, DEC_ROWS)
                                + o_win[g * rows:(g + 1) * rows] * _group_gates(gbr, g, 2, DEC_ROWS))
            for jb, blk in enumerate(_ungroup(o_groups, DEC_ROWS)):
                o_ref[e, :, jb * LANES:(jb + 1) * LANES] = blk

            rolled = pltpu.roll(wins_ref[e], wbuf - n_new, 1)
            tail = jnp.concatenate([jnp.zeros((LANES - DEC_ROWS, KVW), F32), winn_ref[e]], axis=0)
            tail_t = pltpu.roll(tail, DEC_ROWS - n_new, 0).T
            lane = lax.broadcasted_iota(jnp.int32, (KVW, LANES), 1)
            nws_ref[e, :, 0:wbuf - LANES] = rolled[:, 0:wbuf - LANES]
            nws_ref[e, :, wbuf - LANES:wbuf] = jnp.where(lane >= LANES - n_new, tail_t,
                                                         rolled[:, wbuf - LANES:wbuf])


def _dec_slc_call(page_table, qb, gbr, sel, o_cmp, cache, slc_new, win_state, win_new, layer, past_len,
                  n_new):
    batch, n_pages = page_table.shape
    page_rows = cache.shape[3]
    ppc = min(PAGES_PER_CHUNK, n_pages)
    n_chunks = n_pages // ppc
    chunk_rows = ppc * page_rows
    wbuf = win_state.shape[3]
    rows = GRP * DEC_ROWS
    lanes_per_chunk = chunk_rows // BLK_CMP
    assert lanes_per_chunk <= LANES
    sel_c = sel[..., :n_chunks * lanes_per_chunk].reshape(batch, N_KV_B, DEC_ROWS, n_chunks, lanes_per_chunk)
    sel_c = jnp.transpose(sel_c, (0, 3, 1, 2, 4)).reshape(batch, n_chunks, N_KV_B * DEC_ROWS, lanes_per_chunk)
    sel_c = jnp.pad(sel_c, ((0, 0), (0, 0), (0, 0), (0, LANES - lanes_per_chunk)))
    i = jnp.arange(LANES)[:, None]
    k = jnp.arange(chunk_rows)[None, :]
    emat = ((i % 2 == 0) & (i < lanes_per_chunk) & (k // BLK_SLC == i // 2)).astype(BF16)
    group = _dec_group(batch)
    per_b = lambda r, w: pl.BlockSpec((group, r, w), lambda b, c, pt: (b, 0, 0))
    per_bg = lambda r, w: pl.BlockSpec((group, N_KV_B, r, w), lambda b, c, pt: (b, 0, 0, 0))
    stat = pltpu.VMEM((group, N_KV_B * rows, LANES), F32)
    grid_spec = pltpu.PrefetchScalarGridSpec(
        num_scalar_prefetch=1,
        grid=(batch // group, n_chunks),
        in_specs=[per_b(DEC_ROWS, WB), per_b(DEC_ROWS, LANES),
                  pl.BlockSpec((group, None, N_KV_B * DEC_ROWS, LANES), lambda b, c, pt: (b, c, 0, 0)),
                  per_bg(rows, LANES), pl.BlockSpec(memory_space=pl.ANY), per_b(DEC_ROWS, KVW),
                  pl.BlockSpec((None, group, KVW, wbuf), lambda b, c, pt: (layer, b, 0, 0)),
                  per_b(DEC_ROWS, KVW), pl.BlockSpec(emat.shape, lambda b, c, pt: (0, 0))],
        out_specs=[per_b(DEC_ROWS, WB), per_b(KVW, wbuf)],
        scratch_shapes=_dec_specs(group, ppc, page_rows, KVW) + [
            pltpu.VMEM((group, N_KV_B * rows, LANES), BF16), stat, stat, stat],
    )
    return pl.pallas_call(
        functools.partial(_dec_slc_kernel, layer=layer, pages_per_chunk=ppc, past_len=past_len,
                          n_new=n_new),
        grid_spec=grid_spec,
        out_shape=[jax.ShapeDtypeStruct((batch, DEC_ROWS, WB), F32),
                   jax.ShapeDtypeStruct((batch, KVW, wbuf), F32)],
        compiler_params=_cparams(("arbitrary", "arbitrary")),
        name="dec_slc",
    )(page_table, qb, gbr, sel_c, o_cmp, cache, slc_new, win_state, win_new, emat)


def _out_kernel(x_ref, mod_ref, oa_ref, onsa_ref, ng_ref, wg_ref, wpa_ref, wpb_ref, wout_ref, fg_ref,
                y_ref, *, final):
    x = x_ref[...]
    mod = mod_ref[...]
    ub = _modulated(x, mod, ng_ref[...]).astype(BF16)
    za = jax.nn.silu(_dot(ub, wg_ref[:, 0:WA]))
    zb = jax.nn.silu(_dot(ub, wg_ref[:, WA:WA + WB]))
    gm = jax.nn.sigmoid(_dot(ub, wg_ref[:, WA + WB:WA + WB + 2 * D_MODEL]))
    ya = _dot((oa_ref[...] * za).astype(BF16), wpa_ref[...])
    yb = _dot((onsa_ref[...] * zb).astype(BF16), wpb_ref[...])
    merged = gm[:, 0:D_MODEL] * ya + gm[:, D_MODEL:2 * D_MODEL] * yb
    xn = x + mod[:, 2 * D_MODEL:3 * D_MODEL] * _dot(merged.astype(BF16), wout_ref[...])
    y_ref[...] = _rms(xn, fg_ref[...]) if final else xn


def _out_call(x, mod, o_a, o_nsa, w, final_g, *, tokens_per_mod_row, final):
    n = x.shape[0]
    tm = min(TM, n)
    assert n % tm == 0
    if tokens_per_mod_row is None:
        mod_spec = pl.BlockSpec((tm, 3 * D_MODEL), lambda i: (i, 0))
    else:
        per = tokens_per_mod_row // tm
        mod_spec = pl.BlockSpec((None, 1, 3 * D_MODEL), lambda i: (i // per, 0, 0))
    full = lambda a: pl.BlockSpec(a.shape, lambda i: (0,) * a.ndim)
    row = lambda width: pl.BlockSpec((tm, width), lambda i: (i, 0))
    weights = (w["ng"], w["wg"], w["wpa"], w["wpb"], w["wout"], final_g)
    return pl.pallas_call(
        functools.partial(_out_kernel, final=final),
        grid=(n // tm,),
        in_specs=[row(D_MODEL), mod_spec, row(WA), row(WB)] + [full(a) for a in weights],
        out_specs=row(D_MODEL),
        out_shape=jax.ShapeDtypeStruct((n, D_MODEL), F32),
        compiler_params=_cparams(("arbitrary",)),
        name="out",
    )(x, mod, o_a, o_nsa, *weights)


def _pad_cols(a, width):
    return jnp.pad(a, ((0, 0), (0, width - a.shape[1])))


def _layer_weights(l, norm_g, w_in, q_norm_g, kv_norm_g, w_uq, w_uk, w_uv, nsa_pe, w_pa, w_pb, w_out):
    wi = w_in[l]
    o = 0
    parts = {}
    for name, n in (("cq", Q_LORA), ("ckv", KV_LORA), ("kr", ROPE_A), ("za", WA), ("qb", WB),
                    ("kvb", 3 * KVW), ("gbr", 3 * H_B), ("zb", WB), ("gm", 2 * D_MODEL)):
        parts[name] = wi[:, o:o + n]
        o += n
    win = jnp.concatenate([parts["cq"], parts["ckv"], _pad_cols(parts["kr"], LANES), parts["qb"],
                           parts["kvb"], _pad_cols(parts["gbr"], LANES)], axis=1).astype(BF16)
    wg = jnp.concatenate([parts["za"], parts["zb"], parts["gm"]], axis=1).astype(BF16)
    head_pad = lambda a: jnp.pad(a, ((0, 0), (0, 0), (0, LANES - a.shape[2]))).reshape(a.shape[0], H_A * LANES)
    wuq = head_pad(w_uq[l].reshape(Q_LORA, H_A, NOPE + ROPE_A)).astype(BF16)
    wk1 = head_pad(w_uk[l]).astype(BF16)
    j = jnp.arange(LANES)[:, None]
    col = jnp.arange(H_A * LANES)[None, :]
    wk2 = ((j < ROPE_A) & (col % LANES == NOPE + j)).astype(BF16)
    wv = w_uv[l].reshape(KV_LORA, WA).astype(BF16)
    wuk_dec = jnp.pad(jnp.transpose(w_uk[l], (1, 2, 0)), ((0, 0), (0, LANES - NOPE), (0, 0))).astype(BF16)
    pe1 = 1.0 + nsa_pe[l]
    pew = jnp.concatenate([pe1[0], pe1[0], pe1[1], pe1[1]], axis=1)
    return dict(ng=norm_g[l][None], qng=q_norm_g[l][None], kvng=kv_norm_g[l][None], win=win, wg=wg,
                wuq=wuq, wk1=wk1, wk2=wk2, wv=wv, wuk_dec=wuk_dec, pew=pew,
                wpa=w_pa[l].astype(BF16), wpb=w_pb[l].astype(BF16), wout=w_out[l].astype(BF16))


def _rope_tables(pos):
    pos = pos.astype(F32)[:, None]
    lane = jnp.arange(LANES)[None, :]

    def table(half, active, offset):
        inv = ROPE_THETA ** (-(((lane - offset) % half).astype(F32)) / half)
        ang = pos * inv
        first = ((lane - offset) % (2 * half)) < half
        cos = jnp.where(active, jnp.cos(ang), 1.0)
        sin = jnp.where(active, jnp.where(first, -jnp.sin(ang), jnp.sin(ang)), 0.0)
        return cos, sin

    cq, sq = table(ROPE_A // 2, (lane >= NOPE) & (lane < NOPE + ROPE_A), NOPE)
    ca, sa = table(ROPE_A // 2, lane < ROPE_A, 0)
    cb, sb = table(D_B // 2, lane >= 0, 0)
    return jnp.stack([cq, sq, ca, sa, cb, sb])


def _pad_tokens(a, batch, t):
    a = a.reshape(batch, t, a.shape[-1])
    return jnp.pad(a, ((0, 0), (0, DEC_ROWS - t), (0, 0)))


def kernel(x_prompt, x_sample, c_prompt, c_sample, cache_mla, cache_nsa_cmp, cache_nsa_slc,
           state_nsa_win, page_table, norm_g, w_ada, b_ada, w_in, q_norm_g, kv_norm_g, w_uq, w_uk, w_uv,
           nsa_pe, w_pa, w_pb, w_out, final_g):
    depth = norm_g.shape[0]
    batch, seq, _ = x_prompt.shape
    dbatch, dseq, _ = x_sample.shape
    n_pool, page_rows = cache_mla.shape[1], cache_mla.shape[2]
    past_len = page_table.shape[1] * page_rows
    wbuf = state_nsa_win.shape[2]
    assert dseq <= DEC_ROWS and past_len % BLK_SLC == 0 and seq % BLK_SLC == 0 and wbuf >= dseq

    mods = _mod_call(jnp.concatenate([c_prompt, c_sample], axis=0), w_ada.astype(BF16),
                     b_ada[:, None, :])
    rope_p = _rope_tables(jnp.arange(seq, dtype=jnp.int32))
    rope_s = jnp.tile(_rope_tables(past_len + jnp.arange(dseq, dtype=jnp.int32)), (1, dbatch, 1))
    cache_mla_t = jnp.swapaxes(cache_mla, 2, 3)
    feature_major = lambda a: jnp.transpose(a, (0, 1, 3, 4, 5, 2)).reshape(a.shape[0], a.shape[1], KVW, a.shape[2])
    cache_cmp = feature_major(cache_nsa_cmp)
    cache_slc = feature_major(cache_nsa_slc)
    win_state = feature_major(state_nsa_win)
    fg = final_g[None]

    xp = x_prompt.reshape(batch * seq, D_MODEL)
    xs = x_sample.reshape(dbatch * dseq, D_MODEL)
    outs = [[] for _ in range(8)]
    for l in range(depth):
        w = _layer_weights(l, norm_g, w_in, q_norm_g, kv_norm_g, w_uq, w_uk, w_uv, nsa_pe, w_pa, w_pb,
                           w_out)
        final = l == depth - 1

        mod_p = mods[l, :batch][:, None, :]
        q, mla_t, kmla, vmla_t, qb, cmp_t, slc_t, win_t, slc_k, win_k, gbr_t, kvc = _proj_prompt_call(
            xp, mod_p, rope_p, w, batch, seq)
        n_cmp = seq // BLK_CMP
        nc_pad = -(-n_cmp // LANES) * LANES
        kvc = jnp.pad(kvc.reshape(batch, n_cmp, KVW), ((0, 0), (0, nc_pad - n_cmp), (0, 0)))
        o_a, o_nsa = _attn_t_call(q, kmla.reshape(batch, seq, -1), vmla_t, qb, gbr_t, kvc,
                                  slc_k.reshape(batch, seq, LANES), slc_t,
                                  win_k.reshape(batch, seq, LANES), win_t, batch, seq)
        xp = _out_call(xp, mod_p, o_a, o_nsa, w, fg, tokens_per_mod_row=seq, final=final)
        outs[0].append(mla_t)
        outs[1].append(cmp_t)
        outs[2].append(slc_t)
        outs[3].append(win_t[:, :, seq - min(WINDOW, seq):])

        mod_s = jnp.repeat(mods[l, batch:], dseq, axis=0)
        q, mla, qb, cmp_r, slc_r, win_r, gbr = _proj_call(xs, mod_s, rope_s, w)
        o_a = _dec_mla_call(page_table, _pad_tokens(q, dbatch, dseq), w["wuk_dec"], cache_mla_t,
                            _pad_tokens(mla, dbatch, dseq), w["wv"], l, dseq)
        qb_p = _pad_tokens(qb, dbatch, dseq)
        o_cmp, sel = _dec_cmp_call(page_table, qb_p, cache_cmp, w["pew"], l, past_len)
        o_nsa, new_win_t = _dec_slc_call(page_table, qb_p, _pad_tokens(gbr, dbatch, dseq), sel, o_cmp,
                                         cache_slc, _pad_tokens(slc_r, dbatch, dseq), win_state,
                                         _pad_tokens(win_r, dbatch, dseq), l, past_len, dseq)
        xs = _out_call(xs, mod_s, o_a[:, :dseq].reshape(dbatch * dseq, WA),
                       o_nsa[:, :dseq].reshape(dbatch * dseq, WB), w, fg, tokens_per_mod_row=None,
                       final=final)
        outs[4].append(mla.reshape(dbatch, dseq, LAT))
        outs[5].append(cmp_r.reshape(dbatch, dseq, 2, N_KV_B, D_B))
        outs[6].append(slc_r.reshape(dbatch, dseq, 2, N_KV_B, D_B))
        outs[7].append(new_win_t)

    def token_major(parts):
        a = jnp.stack(parts)
        a = a.reshape(a.shape[0], a.shape[1], 2, N_KV_B, D_B, a.shape[3])
        return jnp.transpose(a, (0, 1, 5, 2, 3, 4))

    return (xp.reshape(batch, seq, D_MODEL), xs.reshape(dbatch, dseq, D_MODEL),
            jnp.swapaxes(jnp.stack(outs[0]), 2, 3), token_major(outs[1]), token_major(outs[2]),
            token_major(outs[3]),
            jnp.stack(outs[4]), jnp.stack(outs[5]), jnp.stack(outs[6]), token_major(outs[7]))
```

```python
import functools

import jax
import jax.numpy as jnp
from jax import lax
from jax.experimental import pallas as pl
from jax.experimental.pallas import tpu as pltpu

F32 = jnp.float32
BF16 = jnp.bfloat16

D_MODEL = 1024
H_A = 8
NOPE = 64
ROPE_A = 32
V_A = 64
Q_LORA = 384
KV_LORA = 256
LAT = KV_LORA + ROPE_A
MLA_SCALE = (NOPE + ROPE_A) ** -0.5
H_B = 8
N_KV_B = 2
GRP = H_B // N_KV_B
D_B = 64
BLK_CMP = 32
BLK_SLC = 64
N_SEL = 8
WINDOW = 512
FORCE_BONUS = 1e4
NSA_SCALE = D_B ** -0.5
ROPE_THETA = 10000.0
EPS = 1e-6
NEG = -1e30
LOWEST = -3e38
LOG2E = 1.4426950408889634
WA = H_A * V_A
WB = H_B * D_B
KVW = 2 * N_KV_B * D_B

LANES = 128
SUBLANES = 8
VMEM_LIMIT = 56 * 1024 * 1024

TM = 512
TQ = 256
DEC_ROWS = 8
PAGES_PER_CHUNK = 16
DEC_GROUP = 4
DEC_RING = 3

C_CQ = 0
C_CKV = Q_LORA
C_KR = C_CKV + KV_LORA
C_QB = C_KR + LANES
C_KVB = C_QB + WB
C_GBR = C_KVB + 3 * KVW
C_END = C_GBR + LANES


def _dot(a, b):
    return jnp.dot(a, b, preferred_element_type=F32)


def _dot_nt(a, b):
    return lax.dot_general(a, b, (((1,), (1,)), ((), ())), preferred_element_type=F32)


def _cparams(sem):
    return pltpu.CompilerParams(dimension_semantics=sem, vmem_limit_bytes=VMEM_LIMIT)


def _rms(x, g):
    return x * lax.rsqrt(jnp.mean(x * x, axis=-1, keepdims=True) + EPS) * g


def _rope_lanes(x, cos, sin_signed, half):
    lane = lax.broadcasted_iota(jnp.int32, x.shape, 1)
    first = (lane % (2 * half)) < half
    swapped = jnp.where(first, pltpu.roll(x, LANES - half, 1), pltpu.roll(x, half, 1))
    return x * cos + swapped * sin_signed


def _modulated(x, mod, ng):
    shift = mod[:, 0:D_MODEL]
    scale = mod[:, D_MODEL:2 * D_MODEL]
    return _rms(x, ng) * (1.0 + scale) + shift


def _mod_kernel(c_ref, w_ref, b_ref, o_ref):
    c = c_ref[...]
    o_ref[...] = _dot(jax.nn.silu(c).astype(BF16), w_ref[...]) + b_ref[...]


def _mod_call(c_all, w_ada, b_ada):
    depth = w_ada.shape[0]
    rows = c_all.shape[0]
    nb = 3 * D_MODEL // D_MODEL
    return pl.pallas_call(
        _mod_kernel,
        grid=(depth, nb),
        in_specs=[
            pl.BlockSpec((rows, D_MODEL), lambda l, j: (0, 0)),
            pl.BlockSpec((None, D_MODEL, D_MODEL), lambda l, j: (l, 0, j)),
            pl.BlockSpec((None, 1, D_MODEL), lambda l, j: (l, 0, j)),
        ],
        out_specs=pl.BlockSpec((None, rows, D_MODEL), lambda l, j: (l, 0, j)),
        out_shape=jax.ShapeDtypeStruct((depth, rows, 3 * D_MODEL), F32),
        compiler_params=_cparams(("arbitrary", "arbitrary")),
        name="mod",
    )(c_all, w_ada, b_ada)


def _proj_kernel(x_ref, mod_ref, rope_ref, ng_ref, qng_ref, kvng_ref, win_ref, wuq_ref,
                 q_out, mla_out, qb_out, cmp_out, slc_out, winr_out, gbr_out):
    x = x_ref[...]
    ub = _modulated(x, mod_ref[...], ng_ref[...]).astype(BF16)

    cq = _dot(ub, win_ref[:, C_CQ:C_CKV])
    q = _dot(_rms(cq, qng_ref[...]).astype(BF16), wuq_ref[...])
    cos_q, sin_q = rope_ref[0], rope_ref[1]
    for h in range(H_A):
        blk = q[:, h * LANES:(h + 1) * LANES]
        q_out[:, h * LANES:(h + 1) * LANES] = _rope_lanes(blk, cos_q, sin_q, ROPE_A // 2).astype(BF16)

    zl = _dot(ub, win_ref[:, C_CKV:C_QB])
    krr = _rope_lanes(zl[:, KV_LORA:KV_LORA + LANES], rope_ref[2], rope_ref[3], ROPE_A // 2)
    mla_out[:, 0:KV_LORA] = _rms(zl[:, 0:KV_LORA], kvng_ref[...])
    mla_out[:, KV_LORA:LAT] = krr[:, 0:ROPE_A]

    cos_b, sin_b = rope_ref[4], rope_ref[5]
    zq = _dot(ub, win_ref[:, C_QB:C_KVB])
    for j in range(WB // LANES):
        blk = zq[:, j * LANES:(j + 1) * LANES]
        qb_out[:, j * LANES:(j + 1) * LANES] = _rope_lanes(blk, cos_b, sin_b, D_B // 2) * NSA_SCALE
    zkv = _dot(ub, win_ref[:, C_KVB:C_GBR])
    for j, ref in enumerate((cmp_out, slc_out, winr_out)):
        k = zkv[:, j * KVW:j * KVW + LANES]
        ref[:, 0:LANES] = _rope_lanes(k, cos_b, sin_b, D_B // 2)
        ref[:, LANES:KVW] = zkv[:, j * KVW + LANES:(j + 1) * KVW]
    gbr_out[...] = jax.nn.sigmoid(_dot(ub, win_ref[:, C_GBR:C_END]))


def _proj_call(x, mod, rope_tab, w):
    n = x.shape[0]
    tm = min(TM, n)
    assert n % tm == 0
    mod_spec = pl.BlockSpec((tm, 3 * D_MODEL), lambda i: (i, 0))
    full = lambda a: pl.BlockSpec(a.shape, lambda i: (0,) * a.ndim)
    row = lambda width: pl.BlockSpec((tm, width), lambda i: (i, 0))
    out_specs = [row(H_A * LANES), row(LAT), row(WB), row(KVW), row(KVW), row(KVW), row(LANES)]
    out_shape = [jax.ShapeDtypeStruct((n, H_A * LANES), BF16), jax.ShapeDtypeStruct((n, LAT), F32),
                 jax.ShapeDtypeStruct((n, WB), F32), jax.ShapeDtypeStruct((n, KVW), F32),
                 jax.ShapeDtypeStruct((n, KVW), F32), jax.ShapeDtypeStruct((n, KVW), F32),
                 jax.ShapeDtypeStruct((n, LANES), F32)]
    weights = (w["ng"], w["qng"], w["kvng"], w["win"], w["wuq"])
    return pl.pallas_call(
        _proj_kernel,
        grid=(n // tm,),
        in_specs=[row(D_MODEL), mod_spec,
                  pl.BlockSpec((6, tm, LANES), lambda i: (0, i, 0))] + [full(a) for a in weights],
        out_specs=out_specs,
        out_shape=out_shape,
        compiler_params=_cparams(("arbitrary",)),
        name="proj",
    )(x, mod, rope_tab, *weights)


def _proj_prompt_kernel(x_ref, mod_ref, rope_ref, ng_ref, qng_ref, kvng_ref, win_ref, wuq_ref, wk1_ref,
                        wk2_ref, wv_ref, pew_ref, q_out, mlat_out, kmla_out, vmlat_out, qb_out, cmpt_out,
                        slct_out, wint_out, slck_out, wink_out, gbrt_out, kvc_out):
    x = x_ref[...]
    tm = x.shape[0]
    ub = _modulated(x, mod_ref[...], ng_ref[...]).astype(BF16)

    cq = _dot(ub, win_ref[:, C_CQ:C_CKV])
    q = _dot(_rms(cq, qng_ref[...]).astype(BF16), wuq_ref[...])
    cos_q, sin_q = rope_ref[0], rope_ref[1]
    for h in range(H_A):
        blk = q[:, h * LANES:(h + 1) * LANES]
        q_out[:, h * LANES:(h + 1) * LANES] = _rope_lanes(blk, cos_q, sin_q, ROPE_A // 2).astype(BF16)

    zl = _dot(ub, win_ref[:, C_CKV:C_QB])
    ckvn = _rms(zl[:, 0:KV_LORA], kvng_ref[...])
    krr = _rope_lanes(zl[:, KV_LORA:KV_LORA + LANES], rope_ref[2], rope_ref[3], ROPE_A // 2)
    mlat_out[0:KV_LORA, :] = ckvn.T
    mlat_out[KV_LORA:LAT, :] = krr.T[0:ROPE_A]
    ckvb = ckvn.astype(BF16)
    kmla_out[...] = (_dot(ckvb, wk1_ref[...]) + _dot(krr.astype(BF16), wk2_ref[...])).astype(BF16)
    vmlat_out[...] = _dot(ckvb, wv_ref[...]).T.astype(BF16)

    cos_b, sin_b = rope_ref[4], rope_ref[5]
    zq = _dot(ub, win_ref[:, C_QB:C_KVB])
    for j in range(WB // LANES):
        blk = zq[:, j * LANES:(j + 1) * LANES]
        qb_out[:, j * LANES:(j + 1) * LANES] = _rope_lanes(blk, cos_b, sin_b, D_B // 2) * NSA_SCALE
    zkv = _dot(ub, win_ref[:, C_KVB:C_GBR])
    branch = []
    for j, ref in enumerate((cmpt_out, slct_out, wint_out)):
        k = _rope_lanes(zkv[:, j * KVW:j * KVW + LANES], cos_b, sin_b, D_B // 2)
        v = zkv[:, j * KVW + LANES:(j + 1) * KVW]
        ref[0:LANES, :] = k.T
        ref[LANES:KVW, :] = v.T
        branch.append((k, v))
    slck_out[...] = branch[1][0].astype(BF16)
    wink_out[...] = branch[2][0].astype(BF16)
    gbrt_out[...] = jax.nn.sigmoid(_dot(ub, win_ref[:, C_GBR:C_END])).T

    pew = pew_ref[...]
    for half, val in enumerate(branch[0]):
        rows = val.reshape(tm // BLK_CMP, BLK_CMP, LANES) * pew[:, half * LANES:(half + 1) * LANES][None]
        kvc_out[:, half * LANES:(half + 1) * LANES] = jnp.sum(rows, axis=1) * (1.0 / BLK_CMP)


def _proj_prompt_call(x, mod, rope_tab, w, batch, seq):
    n = batch * seq
    tm = min(TM, seq)
    assert seq % tm == 0
    nt = seq // tm
    full = lambda a: pl.BlockSpec(a.shape, lambda i: (0,) * a.ndim)
    row = lambda width: pl.BlockSpec((tm, width), lambda i: (i, 0))
    col = lambda feat: pl.BlockSpec((None, feat, tm), lambda i: (i // nt, 0, i % nt))
    fm = lambda feat, dt: jax.ShapeDtypeStruct((batch, feat, seq), dt)
    rm = lambda width, dt: jax.ShapeDtypeStruct((n, width), dt)
    out_specs = [row(H_A * LANES), col(LAT), row(H_A * LANES), col(WA), row(WB), col(KVW), col(KVW),
                 col(KVW), row(LANES), row(LANES), col(LANES),
                 pl.BlockSpec((tm // BLK_CMP, KVW), lambda i: (i, 0))]
    out_shape = [rm(H_A * LANES, BF16), fm(LAT, F32), rm(H_A * LANES, BF16), fm(WA, BF16), rm(WB, F32),
                 fm(KVW, F32), fm(KVW, F32), fm(KVW, F32), rm(LANES, BF16), rm(LANES, BF16),
                 fm(LANES, F32), jax.ShapeDtypeStruct((n // BLK_CMP, KVW), F32)]
    weights = (w["ng"], w["qng"], w["kvng"], w["win"], w["wuq"], w["wk1"], w["wk2"], w["wv"], w["pew"])
    return pl.pallas_call(
        _proj_prompt_kernel,
        grid=(n // tm,),
        in_specs=[row(D_MODEL), pl.BlockSpec((None, 1, 3 * D_MODEL), lambda i: (i // nt, 0, 0)),
                  pl.BlockSpec((6, tm, LANES), lambda i: (0, i % nt, 0))] + [full(a) for a in weights],
        out_specs=out_specs,
        out_shape=out_shape,
        compiler_params=_cparams(("arbitrary",)),
        name="proj_prompt",
    )(x, mod, rope_tab, *weights)


def _flash_init(m_ref, l_ref, acc_ref):
    m_ref[...] = jnp.full(m_ref.shape, NEG, F32)
    l_ref[...] = jnp.zeros(l_ref.shape, F32)
    acc_ref[...] = jnp.zeros(acc_ref.shape, F32)


def _lane_tile(a, width):
    reps = width // a.shape[-1]
    return a if reps == 1 else jnp.concatenate([a] * reps, axis=-1)


def _flash_update(s, mask, v, m_ref, l_ref, acc_ref, idx, scale=1.0, v_is_transposed=False):
    c = scale * LOG2E
    if mask is not None:
        s = jnp.where(mask, s, NEG)
    m_prev = m_ref[idx]
    l_prev = l_ref[idx]
    m_next = jnp.maximum(m_prev, jnp.max(s, axis=-1, keepdims=True))
    p = jnp.exp2((s - _lane_tile(m_next, s.shape[-1])) * c)
    if mask is not None:
        p = jnp.where(mask, p, 0.0)
    alpha = jnp.exp2((m_prev - m_next) * c)
    l_ref[idx] = alpha * l_prev + jnp.sum(p, axis=-1, keepdims=True)
    pv = _dot_nt(p.astype(BF16), v) if v_is_transposed else _dot(p.astype(BF16), v)
    acc_ref[idx] = acc_ref[idx] * _lane_tile(alpha, acc_ref.shape[-1]) + pv
    m_ref[idx] = m_next


def _flash_result(l_ref, acc_ref, idx):
    l = l_ref[idx]
    l = jnp.where(l == 0.0, 1.0, l)
    return acc_ref[idx] / _lane_tile(l, acc_ref.shape[-1])


def _group_queries(qb, g, rows):
    lane = lax.broadcasted_iota(jnp.int32, (rows, LANES), 1)
    keep = (lane // D_B) == g
    parts = []
    for r in range(GRP):
        h = g * GRP + r
        blk = qb[:, (h // 2) * LANES:(h // 2 + 1) * LANES]
        if h % 2 != g:
            blk = pltpu.roll(blk, D_B, 1)
        parts.append(jnp.where(keep, blk, 0.0))
    return jnp.concatenate(parts, axis=0).astype(BF16)


def _group_gates(gbr, g, j, rows):
    parts = []
    for r in range(GRP):
        c = 3 * (g * GRP + r) + j
        parts.append(jnp.broadcast_to(gbr[:, c:c + 1], (rows, LANES)))
    return jnp.concatenate(parts, axis=0)


def _ungroup(o_groups, rows):
    lane = lax.broadcasted_iota(jnp.int32, (rows, LANES), 1)
    blocks = []
    for j in range(WB // LANES):
        g = (2 * j) // GRP
        r0 = (2 * j) % GRP
        p0 = o_groups[g][r0 * rows:(r0 + 1) * rows]
        p1 = o_groups[g][(r0 + 1) * rows:(r0 + 2) * rows]
        if g == 1:
            p0 = pltpu.roll(p0, D_B, 1)
        else:
            p1 = pltpu.roll(p1, D_B, 1)
        blocks.append(jnp.where(lane < D_B, p0, p1))
    return blocks


def _compressed_attention(qg, kvc, qpos, n_cmp):
    nc = kvc.shape[0]
    s = _dot_nt(qg, kvc[:, 0:LANES].astype(BF16))
    c = lax.broadcasted_iota(jnp.int32, s.shape, 1)
    ok = ((c + 1) * BLK_CMP - 1 <= qpos) & (c < n_cmp)
    s = jnp.where(ok, s, NEG)
    p = jnp.where(ok, jnp.exp(s - jnp.max(s, axis=-1, keepdims=True)), 0.0)
    den = jnp.sum(p, axis=-1, keepdims=True)
    p = p / jnp.where(den == 0.0, 1.0, den)
    return p, _dot(p.astype(BF16), kvc[:, LANES:KVW].astype(BF16))


def _select_blocks(imp_cmp, cur, n_blocks):
    nc = imp_cmp.shape[-1]
    lane = lax.broadcasted_iota(jnp.int32, imp_cmp.shape, 1)
    pair = imp_cmp + pltpu.roll(imp_cmp, nc - 1, 1)
    blk = lane // 2
    forced = (blk == 0) | (blk == cur - 1)
    score = jnp.where(blk <= cur, pair + jnp.where(forced, FORCE_BONUS, 0.0), NEG)
    cand = ((lane % 2) == 0) & (blk < n_blocks) & (blk != cur)
    score = jnp.where(cand, score, LOWEST)
    lanef = lane.astype(F32)
    sel = jnp.zeros(imp_cmp.shape, F32)
    for _ in range(N_SEL - 1):
        top = jnp.max(score, axis=-1, keepdims=True)
        first = jnp.min(jnp.where(score == top, lanef, float(nc)), axis=-1, keepdims=True)
        hit = lanef == first
        sel = jnp.where(hit, 1.0, sel)
        score = jnp.where(hit, LOWEST, score)
    return sel


def _flash_t_init(m_ref, l_ref, acc_ref):
    m_ref[...] = jnp.full(m_ref.shape, 0.1 * NEG, F32)
    l_ref[...] = jnp.zeros(l_ref.shape, F32)
    acc_ref[...] = jnp.zeros(acc_ref.shape, F32)


def _flash_t_update(s_t, mask_t, v_t, m_ref, l_ref, acc_ref, scale=1.0):
    c = scale * LOG2E
    if mask_t is not None:
        s_t = jnp.where(mask_t, s_t, NEG)
    m_prev = m_ref[...]
    m_next = jnp.maximum(m_prev, jnp.max(s_t, axis=0, keepdims=True))
    p = jnp.exp2((s_t - m_next) * c)
    alpha = jnp.exp2((m_prev - m_next) * c)
    l_ref[...] = alpha * l_ref[...] + jnp.sum(p, axis=0, keepdims=True)
    acc_ref[...] = acc_ref[...] * alpha + _dot(v_t, p.astype(BF16))
    m_ref[...] = m_next


def _flash_t_result(l_ref, acc_ref):
    l = l_ref[...]
    return acc_ref[...] / jnp.where(l == 0.0, 1.0, l)


def _select_blocks_t(imp_cmp, cur, n_blocks):
    nc = imp_cmp.shape[0]
    row = lax.broadcasted_iota(jnp.int32, imp_cmp.shape, 0)
    pair = imp_cmp + pltpu.roll(imp_cmp, nc - 1, 0)
    blk = row // 2
    forced = (blk == 0) | (blk == cur - 1)
    score = jnp.where(blk <= cur, pair + jnp.where(forced, FORCE_BONUS, 0.0), NEG)
    cand = ((row % 2) == 0) & (blk < n_blocks) & (blk != cur)
    score = jnp.where(cand, score, LOWEST)
    rowf = row.astype(F32)
    sel = jnp.zeros(imp_cmp.shape, F32)
    for _ in range(N_SEL - 1):
        top = jnp.max(score, axis=0, keepdims=True)
        first = jnp.min(jnp.where(score == top, rowf, float(nc)), axis=0, keepdims=True)
        hit = rowf == first
        sel = jnp.where(hit, 1.0, sel)
        score = jnp.where(hit, LOWEST, score)
    return sel


def _attn_t_kernel(q_ref, kmla_ref, vmlat_ref, qb_ref, gbrt_ref, kvc_ref, slck_ref, slct_ref, wink_ref,
                   wint_ref, oa_ref, onsa_ref, m_a, l_a, acc_a, qg_s, oc_s, m_s, l_s, acc_s, m_w, l_w, acc_w,
                   *, seq_len):
    tq = q_ref.shape[0]
    bk = tq
    rq = GRP * tq
    qi = pl.program_id(1)
    q0 = qi * tq
    qpos = q0 + lax.broadcasted_iota(jnp.int32, (1, tq), 1)
    qpos4 = q0 + lax.broadcasted_iota(jnp.int32, (1, rq), 1) % tq
    kiota = lax.broadcasted_iota(jnp.int32, (bk, 1), 0)

    _flash_t_init(m_a, l_a, acc_a)

    def mla_body(kb, carry, diagonal=False):
        k0 = pl.multiple_of(kb * bk, bk)
        mask = (k0 + kiota) <= qpos if diagonal else None
        scores = [_dot_nt(kmla_ref[pl.ds(k0, bk), h * LANES:(h + 1) * LANES],
                          q_ref[:, h * LANES:(h + 1) * LANES]) for h in range(H_A)]
        for h in range(H_A):
            v_t = vmlat_ref[h * V_A:(h + 1) * V_A, pl.ds(k0, bk)]
            _flash_t_update(scores[h], mask, v_t, m_a.at[h], l_a.at[h], acc_a.at[h * V_A:(h + 1) * V_A],
                            scale=MLA_SCALE)
        return carry

    lax.fori_loop(0, qi, mla_body, 0)
    mla_body(qi, 0, diagonal=True)
    o_a_t = jnp.concatenate([_flash_t_result(l_a.at[h], acc_a.at[h * V_A:(h + 1) * V_A])
                             for h in range(H_A)], axis=0)
    oa_ref[...] = o_a_t.T

    qb = qb_ref[...]
    kvc = kvc_ref[...]
    kc = kvc[:, 0:LANES].astype(BF16)
    vc_t = kvc[:, LANES:KVW].T.astype(BF16)
    n_cmp = seq_len // BLK_CMP
    n_slc = -(-seq_len // BLK_SLC)
    crow = lax.broadcasted_iota(jnp.int32, (kvc.shape[0], rq), 0)
    c_ok = ((crow + 1) * BLK_CMP - 1 <= qpos4) & (crow < n_cmp)
    sels = []
    for g in range(N_KV_B):
        qg = _group_queries(qb, g, tq)
        qg_s[g] = qg
        s_t = jnp.where(c_ok, _dot_nt(kc, qg), NEG)
        p = jnp.where(c_ok, jnp.exp(s_t - jnp.max(s_t, axis=0, keepdims=True)), 0.0)
        den = jnp.sum(p, axis=0, keepdims=True)
        p = p / jnp.where(den == 0.0, 1.0, den)
        oc_s[g] = _dot(vc_t[g * D_B:(g + 1) * D_B], p.astype(BF16))
        imp = p[:, 0:tq]
        for r in range(1, GRP):
            imp = imp + p[:, r * tq:(r + 1) * tq]
        cur = qpos // BLK_SLC
        sel = _select_blocks_t(imp, cur, n_slc)
        srow = lax.broadcasted_iota(jnp.int32, sel.shape, 0)
        sels.append(jnp.where(srow == 2 * cur, 1.0, sel).astype(BF16))

    _flash_t_init(m_s, l_s, acc_s)

    def slc_body(kb, carry, diagonal=False):
        k0 = pl.multiple_of(kb * bk, bk)
        kblk = slck_ref[pl.ds(k0, bk), :]
        v_t = slct_ref[LANES:KVW, pl.ds(k0, bk)].astype(BF16)
        kk = lax.broadcasted_iota(jnp.int32, (bk, kvc.shape[0]), 0) + k0
        ci = lax.broadcasted_iota(jnp.int32, (bk, kvc.shape[0]), 1)
        in_block = jnp.where(((ci % 2) == 0) & ((kk // BLK_SLC) == (ci // 2)), 1.0, 0.0).astype(BF16)
        masks = []
        for g in range(N_KV_B):
            mask = _dot(in_block, sels[g]) > 0.5
            masks.append(mask & ((k0 + kiota) <= qpos) if diagonal else mask)
        head_update(kblk, v_t, masks, m_s, l_s, acc_s)
        return carry

    def head_update(kblk, v_t, masks, m_ref, l_ref, acc_ref):
        scores = [_dot_nt(kblk, qg_s[h // GRP, (h % GRP) * tq:(h % GRP + 1) * tq]) for h in range(H_B)]
        for h in range(H_B):
            g = h // GRP
            _flash_t_update(scores[h], masks[g], v_t[g * D_B:(g + 1) * D_B], m_ref.at[h], l_ref.at[h],
                            acc_ref.at[h * D_B:(h + 1) * D_B])

    lax.fori_loop(0, qi, slc_body, 0)
    slc_body(qi, 0, diagonal=True)

    _flash_t_init(m_w, l_w, acc_w)

    def win_body(kb, carry, diagonal=False):
        k0 = pl.multiple_of(kb * bk, bk)
        kpos = k0 + kiota
        mask = kpos <= qpos if diagonal else kpos > qpos - WINDOW
        kblk = wink_ref[pl.ds(k0, bk), :]
        v_t = wint_ref[LANES:KVW, pl.ds(k0, bk)].astype(BF16)
        head_update(kblk, v_t, [mask] * N_KV_B, m_w, l_w, acc_w)
        return carry

    lo = jnp.maximum(q0 - (WINDOW - 1), 0) // bk
    lax.fori_loop(lo, qi, win_body, 0)
    win_body(qi, 0, diagonal=True)

    pieces = []
    for h in range(H_B):
        g, r = h // GRP, h % GRP
        rows = slice(h * D_B, (h + 1) * D_B)
        pieces.append(oc_s[g, :, r * tq:(r + 1) * tq] * gbrt_ref[3 * h:3 * h + 1, :]
                      + _flash_t_result(l_s.at[h], acc_s.at[rows]) * gbrt_ref[3 * h + 1:3 * h + 2, :]
                      + _flash_t_result(l_w.at[h], acc_w.at[rows]) * gbrt_ref[3 * h + 2:3 * h + 3, :])
    onsa_ref[...] = jnp.concatenate(pieces, axis=0).T


def _attn_t_call(q, kmla, vmlat, qb, gbrt, kvc, slck, slct, wink, wint, batch, seq_len):
    tq = min(TQ, seq_len)
    assert seq_len % tq == 0 and tq <= WINDOW
    nq = seq_len // tq
    n = batch * seq_len
    nc_pad = kvc.shape[1]
    rq = GRP * tq
    tile = lambda width: pl.BlockSpec((tq, width), lambda b, i: (b * nq + i, 0))
    rows = lambda width: pl.BlockSpec((None, seq_len, width), lambda b, i: (b, 0, 0))
    cols = lambda feat: pl.BlockSpec((None, feat, seq_len), lambda b, i: (b, 0, 0))
    stat = lambda sets, r: pltpu.VMEM((sets, 1, r), F32)
    return pl.pallas_call(
        functools.partial(_attn_t_kernel, seq_len=seq_len),
        grid=(batch, nq),
        in_specs=[tile(H_A * LANES), rows(H_A * LANES), cols(WA), tile(WB),
                  pl.BlockSpec((None, LANES, tq), lambda b, i: (b, 0, i)),
                  pl.BlockSpec((None, nc_pad, KVW), lambda b, i: (b, 0, 0)),
                  rows(LANES), cols(KVW), rows(LANES), cols(KVW)],
        out_specs=[tile(WA), tile(WB)],
        out_shape=[jax.ShapeDtypeStruct((n, WA), F32), jax.ShapeDtypeStruct((n, WB), F32)],
        scratch_shapes=[stat(H_A, tq), stat(H_A, tq), pltpu.VMEM((WA, tq), F32),
                        pltpu.VMEM((N_KV_B, rq, LANES), BF16), pltpu.VMEM((N_KV_B, D_B, rq), F32),
                        stat(H_B, tq), stat(H_B, tq), pltpu.VMEM((WB, tq), F32),
                        stat(H_B, tq), stat(H_B, tq), pltpu.VMEM((WB, tq), F32)],
        compiler_params=_cparams(("arbitrary", "arbitrary")),
        name="attn",
    )(q, kmla, vmlat, qb, gbrt, kvc, slck, slct, wink, wint)


def _page_copy(pt_ref, cache_ref, buf, sem, layer, bg, chunk, slot, e, i, pages_per_chunk):
    group, page_rows = buf.shape[1], cache_ref.shape[3]
    page = pt_ref[bg * group + e, chunk * pages_per_chunk + i]
    return pltpu.make_async_copy(cache_ref.at[layer, page],
                                 buf.at[slot, e, :, pl.ds(i * page_rows, page_rows)], sem.at[slot])


def _stream_chunk(pt_ref, cache_ref, buf, sem, layer, pages_per_chunk):
    bg = pl.program_id(0)
    c = pl.program_id(1)
    total = pl.num_programs(0) * pl.num_programs(1)
    ncnk = pl.num_programs(1)
    ring, group = buf.shape[0], buf.shape[1]
    step = bg * ncnk + c
    slot = step % ring

    def copies(s):
        return [_page_copy(pt_ref, cache_ref, buf, sem, layer, s // ncnk, s % ncnk, s % ring, e, i,
                           pages_per_chunk)
                for e in range(group) for i in range(pages_per_chunk)]

    @pl.when(step == 0)
    def _():
        for ahead in range(ring - 1):
            @pl.when(step + ahead < total)
            def _():
                for cp in copies(step + ahead):
                    cp.start()

    @pl.when(step + ring - 1 < total)
    def _():
        for cp in copies(step + ring - 1):
            cp.start()

    for cp in copies(step):
        cp.wait()
    return slot


def _dec_group(batch):
    return next(g for g in (DEC_GROUP, 2, 1) if batch % g == 0)


def _new_row_mask(rows, n_new):
    j = lax.broadcasted_iota(jnp.int32, (rows, LANES), 1)
    t = lax.broadcasted_iota(jnp.int32, (rows, LANES), 0) % DEC_ROWS
    return (j <= t) & (j < n_new)


def _pad_rows(a, rows):
    return jnp.concatenate([a, jnp.zeros((rows - a.shape[0], a.shape[1]), a.dtype)], axis=0)


def _dec_mla_kernel(pt_ref, q_ref, wuk_ref, cache_ref, new_ref, wuv_ref, o_ref,
                    buf, sem, qlat, qrope, m_r, l_r, acc_r, *, layer, pages_per_chunk, n_new):
    c = pl.program_id(1)
    slot = _stream_chunk(pt_ref, cache_ref, buf, sem, layer, pages_per_chunk)
    group = buf.shape[1]
    rows = H_A * DEC_ROWS

    @pl.when(c == 0)
    def _():
        for e in range(group):
            for h in range(H_A):
                qh = q_ref[e, :, h * LANES:(h + 1) * LANES]
                qlat[e, h * DEC_ROWS:(h + 1) * DEC_ROWS] = _dot(qh, wuk_ref[h])
                qrope[e, h * DEC_ROWS:(h + 1) * DEC_ROWS] = pltpu.roll(qh.astype(F32), LANES - NOPE, 1)
        _flash_init(m_r, l_r, acc_r)

    ql = [qlat[e].astype(BF16) for e in range(group)]
    qr = [qrope[e, :, 0:ROPE_A].astype(BF16) for e in range(group)]

    lat_t = [buf[slot, e].astype(BF16) for e in range(group)]
    scores = [_dot(ql[e], lat_t[e][0:KV_LORA]) + _dot(qr[e], lat_t[e][KV_LORA:LAT]) for e in range(group)]
    for e in range(group):
        _flash_update(scores[e], None, lat_t[e][0:KV_LORA], m_r, l_r, acc_r, e, scale=MLA_SCALE,
                      v_is_transposed=True)

    @pl.when(c == pl.num_programs(1) - 1)
    def _():
        for e in range(group):
            latn = _pad_rows(new_ref[e], LANES).astype(BF16)
            s_new = _dot_nt(ql[e], latn[:, 0:KV_LORA]) + _dot_nt(qr[e], latn[:, KV_LORA:LAT])
            _flash_update(s_new, _new_row_mask(rows, n_new), latn[:, 0:KV_LORA], m_r, l_r, acc_r, e,
                          scale=MLA_SCALE)
            o_lat = _flash_result(l_r, acc_r, e)
            x = _dot(o_lat.astype(BF16), wuv_ref[...]).reshape(H_A, DEC_ROWS, WA)
            head = lax.broadcasted_iota(jnp.int32, x.shape, 0)
            col = lax.broadcasted_iota(jnp.int32, x.shape, 2)
            o_ref[e] = jnp.sum(jnp.where(col // V_A == head, x, 0.0), axis=0)


def _dec_specs(group, pages_per_chunk, page_rows, width):
    return [pltpu.VMEM((DEC_RING, group, width, pages_per_chunk * page_rows), F32),
            pltpu.SemaphoreType.DMA((DEC_RING,))]


def _dec_mla_call(page_table, q, wuk, cache, new_rows, wuv, layer, n_new):
    batch, n_pages = page_table.shape
    page_rows = cache.shape[3]
    ppc = min(PAGES_PER_CHUNK, n_pages)
    assert n_pages % ppc == 0
    group = _dec_group(batch)
    rows = H_A * DEC_ROWS
    per_b = lambda r, w: pl.BlockSpec((group, r, w), lambda b, c, pt: (b, 0, 0))
    full = lambda a: pl.BlockSpec(a.shape, lambda b, c, pt: (0,) * a.ndim)
    grid_spec = pltpu.PrefetchScalarGridSpec(
        num_scalar_prefetch=1,
        grid=(batch // group, n_pages // ppc),
        in_specs=[per_b(DEC_ROWS, H_A * LANES), full(wuk), pl.BlockSpec(memory_space=pl.ANY),
                  per_b(DEC_ROWS, LAT), full(wuv)],
        out_specs=per_b(DEC_ROWS, WA),
        scratch_shapes=_dec_specs(group, ppc, page_rows, LAT) + [
            pltpu.VMEM((group, rows, KV_LORA), F32), pltpu.VMEM((group, rows, LANES), F32),
            pltpu.VMEM((group, rows, LANES), F32), pltpu.VMEM((group, rows, LANES), F32),
            pltpu.VMEM((group, rows, KV_LORA), F32)],
    )
    return pl.pallas_call(
        functools.partial(_dec_mla_kernel, layer=layer, pages_per_chunk=ppc, n_new=n_new),
        grid_spec=grid_spec,
        out_shape=jax.ShapeDtypeStruct((batch, DEC_ROWS, WA), F32),
        compiler_params=_cparams(("arbitrary", "arbitrary")),
        name="dec_mla",
    )(page_table, q, wuk, cache, new_rows, wuv)


def _dec_cmp_kernel(pt_ref, qb_ref, cache_ref, pewt_ref, pool_ref, oc_ref, sel_ref, buf, sem, kvc_s,
                    *, layer, pages_per_chunk, past_len):
    c = pl.program_id(1)
    slot = _stream_chunk(pt_ref, cache_ref, buf, sem, layer, pages_per_chunk)
    group, chunk_rows = buf.shape[1], buf.shape[3]
    blocks = chunk_rows // BLK_CMP
    n_cmp = past_len // BLK_CMP

    @pl.when(c == 0)
    def _():
        kvc_s[...] = jnp.zeros(kvc_s.shape, F32)

    pewt = _lane_tile(pewt_ref[...], chunk_rows)
    pool = pool_ref[...]
    for e in range(group):
        xw = buf[slot, e] * pewt
        hi = xw.astype(BF16)
        mid = (xw - hi.astype(F32)).astype(BF16)
        sums = _dot_nt(pool, hi) + _dot_nt(pool, mid)
        kvc_s[e, pl.ds(pl.multiple_of(c * blocks, SUBLANES), blocks)] = sums * (1.0 / BLK_CMP)

    @pl.when(c == pl.num_programs(1) - 1)
    def _():
        t = lax.broadcasted_iota(jnp.int32, (DEC_ROWS, 1), 0)
        qpos = past_len + t
        qpos4 = jnp.concatenate([qpos] * GRP, axis=0)
        imps = []
        for e in range(group):
            qb = qb_ref[e]
            kvc = kvc_s[e]
            for g in range(N_KV_B):
                p, o = _compressed_attention(_group_queries(qb, g, DEC_ROWS), kvc, qpos4, n_cmp)
                oc_ref[e, g] = o
                imp = p[0:DEC_ROWS]
                for r in range(1, GRP):
                    imp = imp + p[r * DEC_ROWS:(r + 1) * DEC_ROWS]
                imps.append(imp)
        n_sets = group * N_KV_B
        cur = jnp.concatenate([qpos // BLK_SLC] * n_sets, axis=0)
        sel = _select_blocks(jnp.concatenate(imps, axis=0), cur, n_cmp // (BLK_SLC // BLK_CMP))
        for e in range(group):
            for g in range(N_KV_B):
                i = e * N_KV_B + g
                sel_ref[e, g] = sel[i * DEC_ROWS:(i + 1) * DEC_ROWS]


def _dec_cmp_call(page_table, qb, cache, pew, layer, past_len):
    batch, n_pages = page_table.shape
    page_rows = cache.shape[3]
    ppc = min(PAGES_PER_CHUNK, n_pages)
    chunk_rows = ppc * page_rows
    assert n_pages % ppc == 0 and (chunk_rows // BLK_CMP) % SUBLANES == 0 and LANES % BLK_CMP == 0
    group = _dec_group(batch)
    nc_pad = -(-(past_len // BLK_CMP) // LANES) * LANES
    pewt = jnp.tile(pew.T, (1, LANES // BLK_CMP))
    pool = (jnp.arange(chunk_rows)[None, :] // BLK_CMP
            == jnp.arange(chunk_rows // BLK_CMP)[:, None]).astype(BF16)
    per_b = lambda r, w: pl.BlockSpec((group, r, w), lambda b, c, pt: (b, 0, 0))
    per_bg = lambda r, w: pl.BlockSpec((group, N_KV_B, r, w), lambda b, c, pt: (b, 0, 0, 0))
    grid_spec = pltpu.PrefetchScalarGridSpec(
        num_scalar_prefetch=1,
        grid=(batch // group, n_pages // ppc),
        in_specs=[per_b(DEC_ROWS, WB), pl.BlockSpec(memory_space=pl.ANY),
                  pl.BlockSpec(pewt.shape, lambda b, c, pt: (0, 0)),
                  pl.BlockSpec(pool.shape, lambda b, c, pt: (0, 0))],
        out_specs=[per_bg(GRP * DEC_ROWS, LANES), per_bg(DEC_ROWS, nc_pad)],
        scratch_shapes=_dec_specs(group, ppc, page_rows, KVW) + [pltpu.VMEM((group, nc_pad, KVW), F32)],
    )
    return pl.pallas_call(
        functools.partial(_dec_cmp_kernel, layer=layer, pages_per_chunk=ppc, past_len=past_len),
        grid_spec=grid_spec,
        out_shape=[jax.ShapeDtypeStruct((batch, N_KV_B, GRP * DEC_ROWS, LANES), F32),
                   jax.ShapeDtypeStruct((batch, N_KV_B, DEC_ROWS, nc_pad), F32)],
        compiler_params=_cparams(("arbitrary", "arbitrary")),
        name="dec_cmp",
    )(page_table, qb, cache, pewt, pool)


def _dec_slc_kernel(pt_ref, qb_ref, gbr_ref, sel_ref, oc_ref, cache_ref, slcn_ref, wins_ref, winn_ref,
                    emat_ref, o_ref, nws_ref, buf, sem, qall, m_s, l_s, acc_s,
                    *, layer, pages_per_chunk, past_len, n_new):
    c = pl.program_id(1)
    slot = _stream_chunk(pt_ref, cache_ref, buf, sem, layer, pages_per_chunk)
    group = buf.shape[1]
    rows = GRP * DEC_ROWS
    sets = N_KV_B * DEC_ROWS

    @pl.when(c == 0)
    def _():
        for e in range(group):
            qb = qb_ref[e]
            for g in range(N_KV_B):
                qall[e, g * rows:(g + 1) * rows] = _group_queries(qb, g, DEC_ROWS)
        _flash_init(m_s, l_s, acc_s)

    def per_head(per_token):
        parts = []
        for g in range(N_KV_B):
            parts += [per_token[g * DEC_ROWS:(g + 1) * DEC_ROWS]] * GRP
        return jnp.concatenate(parts, axis=0)

    q = [qall[e] for e in range(group)]
    chosen = _dot(sel_ref[...].reshape(group * sets, LANES).astype(BF16), emat_ref[...])
    kv_t = [buf[slot, e].astype(BF16) for e in range(group)]
    scores = [_dot(q[e], kv_t[e][0:LANES]) for e in range(group)]
    for e in range(group):
        mask = per_head(chosen[e * sets:(e + 1) * sets]) > 0.5
        _flash_update(scores[e], mask, kv_t[e][LANES:KVW], m_s, l_s, acc_s, e, v_is_transposed=True)

    @pl.when(c == pl.num_programs(1) - 1)
    def _():
        new_mask = _new_row_mask(N_KV_B * rows, n_new)
        wbuf = wins_ref.shape[2]
        j = lax.broadcasted_iota(jnp.int32, (N_KV_B * rows, wbuf), 1)
        t = lax.broadcasted_iota(jnp.int32, (N_KV_B * rows, wbuf), 0) % DEC_ROWS
        kpos = past_len - wbuf + j
        state_mask = (kpos > past_len + t - WINDOW) & (kpos >= 0)
        o_sel = []
        for e in range(group):
            slcn = _pad_rows(slcn_ref[e], LANES).astype(BF16)
            _flash_update(_dot_nt(q[e], slcn[:, 0:LANES]), new_mask, slcn[:, LANES:KVW], m_s, l_s, acc_s, e)
            o_sel.append(_flash_result(l_s, acc_s, e))
        _flash_init(m_s, l_s, acc_s)
        for e in range(group):
            gbr = gbr_ref[e]
            wins_t = wins_ref[e].astype(BF16)
            winn = _pad_rows(winn_ref[e], LANES).astype(BF16)
            _flash_update(_dot(q[e], wins_t[0:LANES]), state_mask, wins_t[LANES:KVW], m_s, l_s, acc_s, e,
                          v_is_transposed=True)
            _flash_update(_dot_nt(q[e], winn[:, 0:LANES]), new_mask, winn[:, LANES:KVW], m_s, l_s, acc_s, e)
            o_win = _flash_result(l_s, acc_s, e)
            o_groups = []
            for g in range(N_KV_B):
                o_groups.append(oc_ref[e, g] * _group_gates(gbr, g, 0, DEC_ROWS)
                                + o_sel[e][g * rows:(g + 1) * rows] * _group_gates(gbr, g, 1, DEC_ROWS)
                                + o_win[g * rows:(g + 1) * rows] * _group_gates(gbr, g, 2, DEC_ROWS))
            for jb, blk in enumerate(_ungroup(o_groups, DEC_ROWS)):
                o_ref[e, :, jb * LANES:(jb + 1) * LANES] = blk

            rolled = pltpu.roll(wins_ref[e], wbuf - n_new, 1)
            tail = jnp.concatenate([jnp.zeros((LANES - DEC_ROWS, KVW), F32), winn_ref[e]], axis=0)
            tail_t = pltpu.roll(tail, DEC_ROWS - n_new, 0).T
            lane = lax.broadcasted_iota(jnp.int32, (KVW, LANES), 1)
            nws_ref[e, :, 0:wbuf - LANES] = rolled[:, 0:wbuf - LANES]
            nws_ref[e, :, wbuf - LANES:wbuf] = jnp.where(lane >= LANES - n_new, tail_t,
                                                         rolled[:, wbuf - LANES:wbuf])


def _dec_slc_call(page_table, qb, gbr, sel, o_cmp, cache, slc_new, win_state, win_new, layer, past_len,
                  n_new):
    batch, n_pages = page_table.shape
    page_rows = cache.shape[3]
    ppc = min(PAGES_PER_CHUNK, n_pages)
    n_chunks = n_pages // ppc
    chunk_rows = ppc * page_rows
    wbuf = win_state.shape[3]
    rows = GRP * DEC_ROWS
    lanes_per_chunk = chunk_rows // BLK_CMP
    assert lanes_per_chunk <= LANES
    sel_c = sel[..., :n_chunks * lanes_per_chunk].reshape(batch, N_KV_B, DEC_ROWS, n_chunks, lanes_per_chunk)
    sel_c = jnp.transpose(sel_c, (0, 3, 1, 2, 4)).reshape(batch, n_chunks, N_KV_B * DEC_ROWS, lanes_per_chunk)
    sel_c = jnp.pad(sel_c, ((0, 0), (0, 0), (0, 0), (0, LANES - lanes_per_chunk)))
    i = jnp.arange(LANES)[:, None]
    k = jnp.arange(chunk_rows)[None, :]
    emat = ((i % 2 == 0) & (i < lanes_per_chunk) & (k // BLK_SLC == i // 2)).astype(BF16)
    group = _dec_group(batch)
    per_b = lambda r, w: pl.BlockSpec((group, r, w), lambda b, c, pt: (b, 0, 0))
    per_bg = lambda r, w: pl.BlockSpec((group, N_KV_B, r, w), lambda b, c, pt: (b, 0, 0, 0))
    stat = pltpu.VMEM((group, N_KV_B * rows, LANES), F32)
    grid_spec = pltpu.PrefetchScalarGridSpec(
        num_scalar_prefetch=1,
        grid=(batch // group, n_chunks),
        in_specs=[per_b(DEC_ROWS, WB), per_b(DEC_ROWS, LANES),
                  pl.BlockSpec((group, None, N_KV_B * DEC_ROWS, LANES), lambda b, c, pt: (b, c, 0, 0)),
                  per_bg(rows, LANES), pl.BlockSpec(memory_space=pl.ANY), per_b(DEC_ROWS, KVW),
                  pl.BlockSpec((None, group, KVW, wbuf), lambda b, c, pt: (layer, b, 0, 0)),
                  per_b(DEC_ROWS, KVW), pl.BlockSpec(emat.shape, lambda b, c, pt: (0, 0))],
        out_specs=[per_b(DEC_ROWS, WB), per_b(KVW, wbuf)],
        scratch_shapes=_dec_specs(group, ppc, page_rows, KVW) + [
            pltpu.VMEM((group, N_KV_B * rows, LANES), BF16), stat, stat, stat],
    )
    return pl.pallas_call(
        functools.partial(_dec_slc_kernel, layer=layer, pages_per_chunk=ppc, past_len=past_len,
                          n_new=n_new),
        grid_spec=grid_spec,
        out_shape=[jax.ShapeDtypeStruct((batch, DEC_ROWS, WB), F32),
                   jax.ShapeDtypeStruct((batch, KVW, wbuf), F32)],
        compiler_params=_cparams(("arbitrary", "arbitrary")),
        name="dec_slc",
    )(page_table, qb, gbr, sel_c, o_cmp, cache, slc_new, win_state, win_new, emat)


def _out_kernel(x_ref, mod_ref, oa_ref, onsa_ref, ng_ref, wg_ref, wpa_ref, wpb_ref, wout_ref, fg_ref,
                y_ref, *, final):
    x = x_ref[...]
    mod = mod_ref[...]
    ub = _modulated(x, mod, ng_ref[...]).astype(BF16)
    za = jax.nn.silu(_dot(ub, wg_ref[:, 0:WA]))
    zb = jax.nn.silu(_dot(ub, wg_ref[:, WA:WA + WB]))
    gm = jax.nn.sigmoid(_dot(ub, wg_ref[:, WA + WB:WA + WB + 2 * D_MODEL]))
    ya = _dot((oa_ref[...] * za).astype(BF16), wpa_ref[...])
    yb = _dot((onsa_ref[...] * zb).astype(BF16), wpb_ref[...])
    merged = gm[:, 0:D_MODEL] * ya + gm[:, D_MODEL:2 * D_MODEL] * yb
    xn = x + mod[:, 2 * D_MODEL:3 * D_MODEL] * _dot(merged.astype(BF16), wout_ref[...])
    y_ref[...] = _rms(xn, fg_ref[...]) if final else xn


def _out_call(x, mod, o_a, o_nsa, w, final_g, *, tokens_per_mod_row, final):
    n = x.shape[0]
    tm = min(TM, n)
    assert n % tm == 0
    if tokens_per_mod_row is None:
        mod_spec = pl.BlockSpec((tm, 3 * D_MODEL), lambda i: (i, 0))
    else:
        per = tokens_per_mod_row // tm
        mod_spec = pl.BlockSpec((None, 1, 3 * D_MODEL), lambda i: (i // per, 0, 0))
    full = lambda a: pl.BlockSpec(a.shape, lambda i: (0,) * a.ndim)
    row = lambda width: pl.BlockSpec((tm, width), lambda i: (i, 0))
    weights = (w["ng"], w["wg"], w["wpa"], w["wpb"], w["wout"], final_g)
    return pl.pallas_call(
        functools.partial(_out_kernel, final=final),
        grid=(n // tm,),
        in_specs=[row(D_MODEL), mod_spec, row(WA), row(WB)] + [full(a) for a in weights],
        out_specs=row(D_MODEL),
        out_shape=jax.ShapeDtypeStruct((n, D_MODEL), F32),
        compiler_params=_cparams(("arbitrary",)),
        name="out",
    )(x, mod, o_a, o_nsa, *weights)


def _pad_cols(a, width):
    return jnp.pad(a, ((0, 0), (0, width - a.shape[1])))


def _layer_weights(l, norm_g, w_in, q_norm_g, kv_norm_g, w_uq, w_uk, w_uv, nsa_pe, w_pa, w_pb, w_out):
    wi = w_in[l]
    o = 0
    parts = {}
    for name, n in (("cq", Q_LORA), ("ckv", KV_LORA), ("kr", ROPE_A), ("za", WA), ("qb", WB),
                    ("kvb", 3 * KVW), ("gbr", 3 * H_B), ("zb", WB), ("gm", 2 * D_MODEL)):
        parts[name] = wi[:, o:o + n]
        o += n
    win = jnp.concatenate([parts["cq"], parts["ckv"], _pad_cols(parts["kr"], LANES), parts["qb"],
                           parts["kvb"], _pad_cols(parts["gbr"], LANES)], axis=1).astype(BF16)
    wg = jnp.concatenate([parts["za"], parts["zb"], parts["gm"]], axis=1).astype(BF16)
    head_pad = lambda a: jnp.pad(a, ((0, 0), (0, 0), (0, LANES - a.shape[2]))).reshape(a.shape[0], H_A * LANES)
    wuq = head_pad(w_uq[l].reshape(Q_LORA, H_A, NOPE + ROPE_A)).astype(BF16)
    wk1 = head_pad(w_uk[l]).astype(BF16)
    j = jnp.arange(LANES)[:, None]
    col = jnp.arange(H_A * LANES)[None, :]
    wk2 = ((j < ROPE_A) & (col % LANES == NOPE + j)).astype(BF16)
    wv = w_uv[l].reshape(KV_LORA, WA).astype(BF16)
    wuk_dec = jnp.pad(jnp.transpose(w_uk[l], (1, 2, 0)), ((0, 0), (0, LANES - NOPE), (0, 0))).astype(BF16)
    pe1 = 1.0 + nsa_pe[l]
    pew = jnp.concatenate([pe1[0], pe1[0], pe1[1], pe1[1]], axis=1)
    return dict(ng=norm_g[l][None], qng=q_norm_g[l][None], kvng=kv_norm_g[l][None], win=win, wg=wg,
                wuq=wuq, wk1=wk1, wk2=wk2, wv=wv, wuk_dec=wuk_dec, pew=pew,
                wpa=w_pa[l].astype(BF16), wpb=w_pb[l].astype(BF16), wout=w_out[l].astype(BF16))


def _rope_tables(pos):
    pos = pos.astype(F32)[:, None]
    lane = jnp.arange(LANES)[None, :]

    def table(half, active, offset):
        inv = ROPE_THETA ** (-(((lane - offset) % half).astype(F32)) / half)
        ang = pos * inv
        first = ((lane - offset) % (2 * half)) < half
        cos = jnp.where(active, jnp.cos(ang), 1.0)
        sin = jnp.where(active, jnp.where(first, -jnp.sin(ang), jnp.sin(ang)), 0.0)
        return cos, sin

    cq, sq = table(ROPE_A // 2, (lane >= NOPE) & (lane < NOPE + ROPE_A), NOPE)
    ca, sa = table(ROPE_A // 2, lane < ROPE_A, 0)
    cb, sb = table(D_B // 2, lane >= 0, 0)
    return jnp.stack([cq, sq, ca, sa, cb, sb])


def _pad_tokens(a, batch, t):
    a = a.reshape(batch, t, a.shape[-1])
    return jnp.pad(a, ((0, 0), (0, DEC_ROWS - t), (0, 0)))


def kernel(x_prompt, x_sample, c_prompt, c_sample, cache_mla, cache_nsa_cmp, cache_nsa_slc,
           state_nsa_win, page_table, norm_g, w_ada, b_ada, w_in, q_norm_g, kv_norm_g, w_uq, w_uk, w_uv,
           nsa_pe, w_pa, w_pb, w_out, final_g):
    depth = norm_g.shape[0]
    batch, seq, _ = x_prompt.shape
    dbatch, dseq, _ = x_sample.shape
    n_pool, page_rows = cache_mla.shape[1], cache_mla.shape[2]
    past_len = page_table.shape[1] * page_rows
    wbuf = state_nsa_win.shape[2]
    assert dseq <= DEC_ROWS and past_len % BLK_SLC == 0 and seq % BLK_SLC == 0 and wbuf >= dseq

    mods = _mod_call(jnp.concatenate([c_prompt, c_sample], axis=0), w_ada.astype(BF16),
                     b_ada[:, None, :])
    rope_p = _rope_tables(jnp.arange(seq, dtype=jnp.int32))
    rope_s = jnp.tile(_rope_tables(past_len + jnp.arange(dseq, dtype=jnp.int32)), (1, dbatch, 1))
    cache_mla_t = jnp.swapaxes(cache_mla, 2, 3)
    feature_major = lambda a: jnp.transpose(a, (0, 1, 3, 4, 5, 2)).reshape(a.shape[0], a.shape[1], KVW, a.shape[2])
    cache_cmp = feature_major(cache_nsa_cmp)
    cache_slc = feature_major(cache_nsa_slc)
    win_state = feature_major(state_nsa_win)
    fg = final_g[None]

    xp = x_prompt.reshape(batch * seq, D_MODEL)
    xs = x_sample.reshape(dbatch * dseq, D_MODEL)
    outs = [[] for _ in range(8)]
    for l in range(depth):
        w = _layer_weights(l, norm_g, w_in, q_norm_g, kv_norm_g, w_uq, w_uk, w_uv, nsa_pe, w_pa, w_pb,
                           w_out)
        final = l == depth - 1

        mod_p = mods[l, :batch][:, None, :]
        q, mla_t, kmla, vmla_t, qb, cmp_t, slc_t, win_t, slc_k, win_k, gbr_t, kvc = _proj_prompt_call(
            xp, mod_p, rope_p, w, batch, seq)
        n_cmp = seq // BLK_CMP
        nc_pad = -(-n_cmp // LANES) * LANES
        kvc = jnp.pad(kvc.reshape(batch, n_cmp, KVW), ((0, 0), (0, nc_pad - n_cmp), (0, 0)))
        o_a, o_nsa = _attn_t_call(q, kmla.reshape(batch, seq, -1), vmla_t, qb, gbr_t, kvc,
                                  slc_k.reshape(batch, seq, LANES), slc_t,
                                  win_k.reshape(batch, seq, LANES), win_t, batch, seq)
        xp = _out_call(xp, mod_p, o_a, o_nsa, w, fg, tokens_per_mod_row=seq, final=final)
        outs[0].append(mla_t)
        outs[1].append(cmp_t)
        outs[2].append(slc_t)
        outs[3].append(win_t[:, :, seq - min(WINDOW, seq):])

        mod_s = jnp.repeat(mods[l, batch:], dseq, axis=0)
        q, mla, qb, cmp_r, slc_r, win_r, gbr = _proj_call(xs, mod_s, rope_s, w)
        o_a = _dec_mla_call(page_table, _pad_tokens(q, dbatch, dseq), w["wuk_dec"], cache_mla_t,
                            _pad_tokens(mla, dbatch, dseq), w["wv"], l, dseq)
        qb_p = _pad_tokens(qb, dbatch, dseq)
        o_cmp, sel = _dec_cmp_call(page_table, qb_p, cache_cmp, w["pew"], l, past_len)
        o_nsa, new_win_t = _dec_slc_call(page_table, qb_p, _pad_tokens(gbr, dbatch, dseq), sel, o_cmp,
                                         cache_slc, _pad_tokens(slc_r, dbatch, dseq), win_state,
                                         _pad_tokens(win_r, dbatch, dseq), l, past_len, dseq)
        xs = _out_call(xs, mod_s, o_a[:, :dseq].reshape(dbatch * dseq, WA),
                       o_nsa[:, :dseq].reshape(dbatch * dseq, WB), w, fg, tokens_per_mod_row=None,
                       final=final)
        outs[4].append(mla.reshape(dbatch, dseq, LAT))
        outs[5].append(cmp_r.reshape(dbatch, dseq, 2, N_KV_B, D_B))
        outs[6].append(slc_r.reshape(dbatch, dseq, 2, N_KV_B, D_B))
        outs[7].append(new_win_t)

    def token_major(parts):
        a = jnp.stack(parts)
        a = a.reshape(a.shape[0], a.shape[1], 2, N_KV_B, D_B, a.shape[3])
        return jnp.transpose(a, (0, 1, 5, 2, 3, 4))

    return (xp.reshape(batch, seq, D_MODEL), xs.reshape(dbatch, dseq, D_MODEL),
            jnp.swapaxes(jnp.stack(outs[0]), 2, 3), token_major(outs[1]), token_major(outs[2]),
            token_major(outs[3]),
            jnp.stack(outs[4]), jnp.stack(outs[5]), jnp.stack(outs[6]), token_major(outs[7]))
```
